```python
import math
import jax, jax.numpy as jnp
from jax import lax
import numpy as np

D_MODEL = 1024
BATCH = 2
SEQ = 16384
DEPTH = 2
DEC_BATCH = 8
DEC_SEQ = 64
PAST_LEN = 2048

CHUNK = 64
N_META = 16
HEAD_DIM = 64
D_A = 384
H_A = D_A // HEAD_DIM
CONV_W = 4
D_B = 384
H_B = D_B // HEAD_DIM
R_W = 64
R_A = 64
R_G = 128
RWKV_GN_EPS = 64e-5
D_C = 256
C_GROUP = 16
G_C = D_C // C_GROUP
P_C = 64

D_MIX = D_A + D_B + D_C
N_A_COLS = 4 * D_A + 2 * H_A
N_B_COLS = 3 * D_B + R_W + R_A + R_G
N_IN = N_A_COLS + N_B_COLS + D_C
N_GROUPS = 4
EXP_PER_GROUP = 8
N_EXP = N_GROUPS * EXP_PER_GROUP
TOP_K = 2
D_EXPERT = 512
MOE_BLOCK = 256
DEEP_ALPHA = (2 * DEPTH) ** 0.25
DEEP_BETA = (8 * DEPTH) ** -0.25
LN_EPS = 1e-5

kernel_name = 'hymba_mlstm_rwkv7_s5_hmoe_stream_step'


def layer_norm(x, g, b):
    xf = x.astype(jnp.float32)
    mu = xf.mean(-1, keepdims=True)
    var = jnp.square(xf - mu).mean(-1, keepdims=True)
    return ((xf - mu) * lax.rsqrt(var + LN_EPS) * g + b).astype(x.dtype)


def head_norm(h, g, eps):
    mu = h.mean(-1, keepdims=True)
    var = jnp.square(h - mu).mean(-1, keepdims=True)
    hn = (h - mu) * lax.rsqrt(var + eps)
    return hn.reshape(*h.shape[:-2], -1) * g


def causal_conv(z, buf, w):
    L = z.shape[1]
    zz = jnp.concatenate([buf, z], axis=1)
    out = zz[:, CONV_W - 1:] * w[CONV_W - 1]
    for j in range(CONV_W - 1):
        out = out + zz[:, j:j + L] * w[j]
    return out, zz[:, L:]


def token_shift(z, buf, mu):
    zz = jnp.concatenate([buf, z], axis=1)
    return z + mu * (zz[:, :-1] - z), zz[:, -1:]


def mlstm_chunk(carry, xs):
    C0, n0, m0 = carry
    q, k, v, logi, logf = xs
    L = q.shape[2]
    b = jnp.cumsum(logf, axis=-1)
    causal = jnp.tril(jnp.ones((L, L), dtype=bool))
    dmat = jnp.where(causal, b[..., :, None] - b[..., None, :] + logi[..., None, :], -jnp.inf)
    g = b + m0[..., None]
    m = jnp.maximum(g, dmat.max(-1))
    w_inter = jnp.exp(g - m)
    s = jnp.einsum('bhtd,bhsd->bhts', q, k) * jnp.exp(dmat - m[..., None])
    num = w_inter[..., None] * jnp.einsum('bhed,bhtd->bhte', C0, q) + jnp.einsum('bhts,bhse->bhte', s, v)
    den = w_inter * jnp.einsum('bhd,bhtd->bht', n0, q) + s.sum(-1)
    h = num / jnp.maximum(jnp.abs(den), jnp.exp(-m))[..., None]
    m_end = m[..., -1]
    w_end = jnp.exp(b[..., -1:] - b + logi - m_end[..., None])
    dec = jnp.exp(g[..., -1] - m_end)
    C1 = dec[..., None, None] * C0 + jnp.einsum('bhs,bhse,bhsd->bhed', w_end, v, k)
    n1 = dec[..., None] * n0 + jnp.einsum('bhs,bhsd->bhd', w_end, k)
    return (C1, n1, m_end), h


def mlstm_mixer(za, C0, n0, m0, conv_buf, conv_w, gate_b, norm_g, lead):
    B, L, _ = za.shape
    qk, conv_new = causal_conv(za[..., :2 * D_A], conv_buf, conv_w)
    qk = jax.nn.silu(qk)

    def heads(t):
        return t.reshape(B, L, H_A, HEAD_DIM).transpose(0, 2, 1, 3)

    q = heads(qk[..., :D_A])
    k = heads(qk[..., D_A:]) * HEAD_DIM ** -0.5
    v = heads(za[..., 2 * D_A:3 * D_A])
    o = jax.nn.sigmoid(za[..., 3 * D_A:4 * D_A])
    logi = (za[..., 4 * D_A:4 * D_A + H_A] + gate_b[0]).transpose(0, 2, 1)
    logf = jax.nn.log_sigmoid(za[..., 4 * D_A + H_A:] + gate_b[1]).transpose(0, 2, 1)
    carry = (C0, n0, m0)
    carry, h = mlstm_chunk(carry, (q[:, :, :lead], k[:, :, :lead], v[:, :, :lead], logi[..., :lead], logf[..., :lead]))
    rest = L - lead
    if rest > 0:
        nc = rest // CHUNK

        def chunks(t):
            t = t[:, :, lead:]
            return jnp.moveaxis(t.reshape(B, H_A, nc, CHUNK, *t.shape[3:]), 2, 0)

        carry, hr = lax.scan(mlstm_chunk, carry, (chunks(q), chunks(k), chunks(v), chunks(logi), chunks(logf)))
        hr = jnp.moveaxis(hr, 0, 2).reshape(B, H_A, rest, HEAD_DIM)
        h = jnp.concatenate([h, hr], axis=2)
    y = o * head_norm(h.transpose(0, 2, 1, 3), norm_g, LN_EPS)
    C1, n1, m1 = carry
    return y, (C1, n1, m1, conv_new)


def rwkv_step(S, inp):
    r, w, k, v, a_vec, b_vec = inp
    Sa = jnp.einsum('bhij,bhj->bhi', S, a_vec)
    S = S * w[:, :, None, :] + Sa[..., None] * b_vec[:, :, None, :] + v[..., None] * k[:, :, None, :]
    return S, jnp.einsum('bhij,bhj->bhi', S, r)


def rwkv_mixer(zb, S0, shift_buf, mu, vecs, w_up, a_up, g_up):
    B, L, _ = zb.shape
    zs, shift_new = token_shift(zb, shift_buf, mu)
    r = zs[..., :D_B]
    k = zs[..., D_B:2 * D_B]
    v = zs[..., 2 * D_B:3 * D_B]
    o = 3 * D_B
    xw = zs[..., o:o + R_W]
    xa = zs[..., o + R_W:o + R_W + R_A]
    xg = zs[..., o + R_W + R_A:]
    w0, a0, k_k, k_a, r_k, gn_g = vecs[0], vecs[1], vecs[2], vecs[3], vecs[4], vecs[5]
    w_log = -jax.nn.softplus(-(w0 + jnp.tanh(xw) @ w_up)) - 0.5
    decay = jnp.exp(-jnp.exp(w_log))
    a = jax.nn.sigmoid(a0 + xa @ a_up)
    g = jax.nn.sigmoid(xg) @ g_up

    def hd(t):
        return t.reshape(B, L, H_B, HEAD_DIM)

    kk = hd(k * k_k)
    kk = kk / jnp.maximum(jnp.linalg.norm(kk, axis=-1, keepdims=True), 1e-12)
    k = k * (1.0 + (a - 1.0) * k_a)
    r_h, k_h, v_h, a_h, d_h = hd(r), hd(k), hd(v), hd(a), hd(decay)

    def tm(t):
        return jnp.moveaxis(t, 1, 0)

    S1, y = lax.scan(rwkv_step, S0, (tm(r_h), tm(d_h), tm(k_h), tm(v_h), tm(-kk), tm(kk * a_h)))
    y = jnp.moveaxis(y, 0, 1)
    bonus = jnp.sum(r_h * k_h * r_k.reshape(H_B, HEAD_DIM), -1, keepdims=True) * v_h
    out = (head_norm(y, gn_g, RWKV_GN_EPS) + bonus.reshape(B, L, D_B)) * g
    return out, (S1, shift_new)


def s5_combine(e1, e2):
    a1r, a1i, b1r, b1i = e1
    a2r, a2i, b2r, b2i = e2
    return (a2r * a1r - a2i * a1i, a2r * a1i + a2i * a1r,
            a2r * b1r - a2i * b1i + b2r, a2r * b1i + a2i * b1r + b2i)


def s5_mixer(u, h_re, h_im, A_re, A_im, log_dt, B_re, B_im, C_re, C_im, D_skip, w_glu):
    Bsz, L, _ = u.shape
    f32 = jnp.float32
    A_re = A_re.astype(f32)
    A_im = A_im.astype(f32)
    dt = jnp.exp(log_dt.astype(f32))[:, None]
    mag = jnp.exp(A_re * dt)
    ab_re = mag * jnp.cos(A_im * dt)
    ab_im = mag * jnp.sin(A_im * dt)
    inv = 1.0 / (A_re * A_re + A_im * A_im)
    q_re = ((ab_re - 1.0) * A_re + ab_im * A_im) * inv
    q_im = (ab_im * A_re - (ab_re - 1.0) * A_im) * inv
    bb_re = q_re[..., None] * B_re - q_im[..., None] * B_im
    bb_im = q_re[..., None] * B_im + q_im[..., None] * B_re
    ug = u.reshape(Bsz, L, G_C, C_GROUP)
    bu_re = jnp.einsum('blgc,gpc->blgp', ug, bb_re)
    bu_im = jnp.einsum('blgc,gpc->blgp', ug, bb_im)
    bu_re = bu_re.at[:, 0].add(ab_re * h_re - ab_im * h_im)
    bu_im = bu_im.at[:, 0].add(ab_re * h_im + ab_im * h_re)
    _, _, x_re, x_im = lax.associative_scan(
        s5_combine,
        (jnp.broadcast_to(ab_re, bu_re.shape), jnp.broadcast_to(ab_im, bu_im.shape), bu_re, bu_im),
        axis=1)
    y = jnp.einsum('blgp,gcp->blgc', x_re, C_re) - jnp.einsum('blgp,gcp->blgc', x_im, C_im)
    y = jax.nn.gelu(y.reshape(Bsz, L, D_C) + D_skip * u)
    return y * jax.nn.sigmoid(y @ w_glu), (x_re[:, -1], x_im[:, -1])


def routed_experts(xt, expert, gates, w_gate, w_up, w_down):
    T, D = xt.shape
    A = T * TOP_K
    n_blocks = -(-A // MOE_BLOCK) + N_EXP
    flat_e = expert.reshape(-1)
    order = jnp.argsort(flat_e)
    se = flat_e[order]
    counts = jnp.bincount(flat_e, length=N_EXP)
    padded = (counts + MOE_BLOCK - 1) // MOE_BLOCK * MOE_BLOCK
    pad_end = jnp.cumsum(padded)
    pad_start = pad_end - padded
    start = jnp.cumsum(counts) - counts
    dest = pad_start[se] + jnp.arange(A) - start[se]
    row_tok = jnp.full((n_blocks * MOE_BLOCK,), T, jnp.int32).at[dest].set((order // TOP_K).astype(jnp.int32))
    row_gate = jnp.zeros((n_blocks * MOE_BLOCK,), jnp.float32).at[dest].set(gates.reshape(-1)[order])
    block_e = jnp.minimum(jnp.searchsorted(pad_end, jnp.arange(n_blocks) * MOE_BLOCK, side='right'), N_EXP - 1)
    xs = jnp.concatenate([xt, jnp.zeros((1, D), xt.dtype)], axis=0)[row_tok].reshape(n_blocks, MOE_BLOCK, D)

    def block_ffn(args):
        xb, e = args
        h = jax.nn.silu(xb @ w_gate[e]) * (xb @ w_up[e])
        return h @ w_down[e]

    ys = lax.map(block_ffn, (xs, block_e)).reshape(-1, D)
    out = jnp.zeros((T + 1, D), jnp.float32).at[row_tok].add(ys * row_gate[:, None])
    return out[:T].astype(xt.dtype)


def hier_moe(x, w_coarse, b_coarse, w_fine, b_fine, w_gate, w_up, w_down):
    B, L, D = x.shape
    xt = x.reshape(-1, D)
    T = xt.shape[0]
    pc = jax.nn.softmax((xt @ w_coarse).astype(jnp.float32) + b_coarse, axis=-1)
    p_grp, grp = lax.top_k(pc, 1)
    lf = ((xt @ w_fine).astype(jnp.float32) + b_fine).reshape(T, N_GROUPS, EXP_PER_GROUP)
    pf = jax.nn.softmax(lf[jnp.arange(T), grp[:, 0]], axis=-1)
    top_p, top_i = lax.top_k(pf, TOP_K)
    gates = p_grp * top_p / top_p.sum(-1, keepdims=True)
    expert = grp * EXP_PER_GROUP + top_i
    return routed_experts(xt, expert, gates, w_gate, w_up, w_down).reshape(B, L, D)


def trunk_layer(x, st, lead, p):
    C0, n0, m0, conv0, S0, sh0, hr0, hi0 = [s.astype(jnp.float32) for s in st]
    z = jnp.matmul(x, p['w_in']).astype(jnp.float32)
    za = z[..., :N_A_COLS]
    zb = z[..., N_A_COLS:N_A_COLS + N_B_COLS]
    zc = z[..., N_A_COLS + N_B_COLS:]
    ya, (C1, n1, m1, conv1) = mlstm_mixer(za, C0, n0, m0, conv0, p['mlstm_conv_w'], p['mlstm_gate_b'], p['mlstm_norm_g'], lead)
    yb, (S1, sh1) = rwkv_mixer(zb, S0, sh0, p['rwkv_mu'], p['rwkv_vecs'], p['rwkv_w_up'], p['rwkv_a_up'], p['rwkv_g_up'])
    yc, (hr1, hi1) = s5_mixer(zc, hr0, hi0, p['s5_A_re'], p['s5_A_im'], p['s5_log_dt'], p['s5_B_re'], p['s5_B_im'],
                              p['s5_C_re'], p['s5_C_im'], p['s5_D'], p['s5_w_glu'])
    mix = jnp.matmul(jnp.concatenate([ya, yb, yc], axis=-1).astype(x.dtype), p['w_out'])
    x = layer_norm(DEEP_ALPHA * x + mix, p['ln_g'][0], p['ln_b'][0])
    ffn = hier_moe(x, p['moe_w_coarse'], p['moe_b_coarse'], p['moe_w_fine'], p['moe_b_fine'],
                   p['moe_w_gate'], p['moe_w_up'], p['moe_w_down'])
    x = layer_norm(DEEP_ALPHA * x + ffn, p['ln_g'][1], p['ln_b'][1])
    return x, (C1, n1, m1, conv1, S1, sh1, hr1, hi1)


def run_trunk(x, states, lead, params):
    new = [[] for _ in states]
    for l in range(DEPTH):
        p = {name: arr[l] for name, arr in params.items()}
        x, st_new = trunk_layer(x, [s[l] for s in states], lead, p)
        for lst, s in zip(new, st_new):
            lst.append(s)
    return x, [jnp.stack(lst) for lst in new]


def fresh_states(batch):
    f = jnp.float32
    return [jnp.zeros((DEPTH, batch, H_A, HEAD_DIM, HEAD_DIM), f),
            jnp.zeros((DEPTH, batch, H_A, HEAD_DIM), f),
            jnp.zeros((DEPTH, batch, H_A), f),
            jnp.zeros((DEPTH, batch, CONV_W - 1, 2 * D_A), f),
            jnp.zeros((DEPTH, batch, H_B, HEAD_DIM, HEAD_DIM), f),
            jnp.zeros((DEPTH, batch, 1, N_B_COLS), f),
            jnp.zeros((DEPTH, batch, G_C, P_C), f),
            jnp.zeros((DEPTH, batch, G_C, P_C), f)]


def setup_inputs(seed: int = 0) -> dict:
    key = jax.random.key(seed)
    keys = iter(jax.random.split(key, 64))
    f32 = jnp.float32

    def nrm(shape, scale=1.0):
        return scale * jax.random.normal(next(keys), shape, f32)

    def unif(shape, lo, hi):
        return jax.random.uniform(next(keys), shape, f32, lo, hi)

    inp = {}
    inp['x_prompt'] = nrm((BATCH, SEQ, D_MODEL))
    inp['x_sample'] = nrm((DEC_BATCH, DEC_SEQ, D_MODEL))
    inp['state_mlstm_C'] = nrm((DEPTH, DEC_BATCH, H_A, HEAD_DIM, HEAD_DIM), 0.1)
    inp['state_mlstm_n'] = nrm((DEPTH, DEC_BATCH, H_A, HEAD_DIM), 0.1)
    inp['state_mlstm_m'] = nrm((DEPTH, DEC_BATCH, H_A))
    inp['state_mlstm_conv'] = nrm((DEPTH, DEC_BATCH, CONV_W - 1, 2 * D_A))
    inp['state_rwkv_S'] = nrm((DEPTH, DEC_BATCH, H_B, HEAD_DIM, HEAD_DIM), 0.1)
    inp['state_rwkv_shift'] = nrm((DEPTH, DEC_BATCH, 1, N_B_COLS))
    inp['state_s5_re'] = nrm((DEPTH, DEC_BATCH, G_C, P_C), 0.1)
    inp['state_s5_im'] = nrm((DEPTH, DEC_BATCH, G_C, P_C), 0.1)
    inp['meta_tokens'] = nrm((N_META, D_MODEL))
    inp['w_in'] = nrm((DEPTH, D_MODEL, N_IN), D_MODEL ** -0.5)
    inp['w_out'] = nrm((DEPTH, D_MIX, D_MODEL), DEEP_BETA * D_MIX ** -0.5)
    inp['mlstm_conv_w'] = nrm((DEPTH, CONV_W, 2 * D_A), CONV_W ** -0.5)
    gate_i = nrm((DEPTH, H_A), 0.1)
    gate_f = jnp.linspace(3.0, 6.0, H_A, dtype=f32) + nrm((DEPTH, H_A), 0.1)
    inp['mlstm_gate_b'] = jnp.stack([gate_i, gate_f], axis=1)
    inp['mlstm_norm_g'] = 1.0 + nrm((DEPTH, D_A), 0.02)
    inp['rwkv_mu'] = unif((DEPTH, N_B_COLS), 0.0, 1.0)
    ramp = jnp.arange(D_B, dtype=f32) / (D_B - 1)
    w0 = -6.5 + 5.0 * ramp ** 0.85 + nrm((DEPTH, D_B), 0.1)
    inp['rwkv_vecs'] = jnp.stack([w0,
                                  nrm((DEPTH, D_B), 0.1),
                                  0.85 + nrm((DEPTH, D_B), 0.05),
                                  1.0 + nrm((DEPTH, D_B), 0.05),
                                  nrm((DEPTH, D_B), 0.1),
                                  1.0 + nrm((DEPTH, D_B), 0.02)], axis=1)
    inp['rwkv_w_up'] = nrm((DEPTH, R_W, D_B), 0.5 * R_W ** -0.5)
    inp['rwkv_a_up'] = nrm((DEPTH, R_A, D_B), 0.5 * R_A ** -0.5)
    inp['rwkv_g_up'] = nrm((DEPTH, R_G, D_B), 2.0 * R_G ** -0.5)
    inp['s5_A_re'] = -0.5 + nrm((DEPTH, G_C, P_C), 0.01)
    inp['s5_A_im'] = jnp.broadcast_to(jnp.pi * jnp.arange(P_C, dtype=f32), (DEPTH, G_C, P_C))
    inp['s5_log_dt'] = unif((DEPTH, G_C), math.log(1e-3), math.log(1e-1))
    inp['s5_B_re'] = nrm((DEPTH, G_C, P_C, C_GROUP), (2 * C_GROUP) ** -0.5)
    inp['s5_B_im'] = nrm((DEPTH, G_C, P_C, C_GROUP), (2 * C_GROUP) ** -0.5)
    inp['s5_C_re'] = nrm((DEPTH, G_C, C_GROUP, P_C), P_C ** -0.5)
    inp['s5_C_im'] = nrm((DEPTH, G_C, C_GROUP, P_C), P_C ** -0.5)
    inp['s5_D'] = nrm((DEPTH, D_C))
    inp['s5_w_glu'] = nrm((DEPTH, D_C, D_C), D_C ** -0.5)
    inp['ln_g'] = 1.0 + nrm((DEPTH, 2, D_MODEL), 0.02)
    inp['ln_b'] = nrm((DEPTH, 2, D_MODEL), 0.02)
    inp['moe_w_coarse'] = nrm((DEPTH, D_MODEL, N_GROUPS), D_MODEL ** -0.5)
    inp['moe_b_coarse'] = nrm((DEPTH, N_GROUPS), 0.01)
    inp['moe_w_fine'] = nrm((DEPTH, D_MODEL, N_EXP), D_MODEL ** -0.5)
    inp['moe_b_fine'] = nrm((DEPTH, N_EXP), 0.01)
    inp['moe_w_gate'] = nrm((DEPTH, N_EXP, D_MODEL, D_EXPERT), D_MODEL ** -0.5)
    inp['moe_w_up'] = nrm((DEPTH, N_EXP, D_MODEL, D_EXPERT), D_MODEL ** -0.5)
    inp['moe_w_down'] = nrm((DEPTH, N_EXP, D_EXPERT, D_MODEL), DEEP_BETA * D_EXPERT ** -0.5)
    return inp


def reference(x_prompt, x_sample, state_mlstm_C, state_mlstm_n, state_mlstm_m, state_mlstm_conv,
              state_rwkv_S, state_rwkv_shift, state_s5_re, state_s5_im, meta_tokens, w_in, w_out,
              mlstm_conv_w, mlstm_gate_b, mlstm_norm_g, rwkv_mu, rwkv_vecs, rwkv_w_up, rwkv_a_up, rwkv_g_up,
              s5_A_re, s5_A_im, s5_log_dt, s5_B_re, s5_B_im, s5_C_re, s5_C_im, s5_D, s5_w_glu,
              ln_g, ln_b, moe_w_coarse, moe_b_coarse, moe_w_fine, moe_b_fine, moe_w_gate, moe_w_up, moe_w_down):
    params = dict(w_in=w_in, w_out=w_out, mlstm_conv_w=mlstm_conv_w, mlstm_gate_b=mlstm_gate_b,
                  mlstm_norm_g=mlstm_norm_g, rwkv_mu=rwkv_mu, rwkv_vecs=rwkv_vecs, rwkv_w_up=rwkv_w_up,
                  rwkv_a_up=rwkv_a_up, rwkv_g_up=rwkv_g_up, s5_A_re=s5_A_re, s5_A_im=s5_A_im,
                  s5_log_dt=s5_log_dt, s5_B_re=s5_B_re, s5_B_im=s5_B_im, s5_C_re=s5_C_re, s5_C_im=s5_C_im,
                  s5_D=s5_D, s5_w_glu=s5_w_glu, ln_g=ln_g, ln_b=ln_b, moe_w_coarse=moe_w_coarse,
                  moe_b_coarse=moe_b_coarse, moe_w_fine=moe_w_fine, moe_b_fine=moe_b_fine,
                  moe_w_gate=moe_w_gate, moe_w_up=moe_w_up, moe_w_down=moe_w_down)
    state_in = [state_mlstm_C, state_mlstm_n, state_mlstm_m, state_mlstm_conv,
                state_rwkv_S, state_rwkv_shift, state_s5_re, state_s5_im]

    b = x_prompt.shape[0]
    meta = jnp.broadcast_to(meta_tokens.astype(x_prompt.dtype)[None], (b, N_META, D_MODEL))
    xp = jnp.concatenate([meta, x_prompt], axis=1)
    yp, pst = run_trunk(xp, fresh_states(b), N_META, params)
    y_prompt = yp[:, N_META:]
    p_mlstm_C, p_mlstm_n, p_mlstm_m, p_mlstm_conv, p_rwkv_S, p_rwkv_shift, p_s5_re, p_s5_im = [
        s.astype(r.dtype) for s, r in zip(pst, state_in)]

    y_sample, sst = run_trunk(x_sample, state_in, x_sample.shape[1], params)
    s_mlstm_C, s_mlstm_n, s_mlstm_m, s_mlstm_conv, s_rwkv_S, s_rwkv_shift, s_s5_re, s_s5_im = [
        s.astype(r.dtype) for s, r in zip(sst, state_in)]

    return (y_prompt, y_sample,
            p_mlstm_C, p_mlstm_n, p_mlstm_m, p_mlstm_conv, p_rwkv_S, p_rwkv_shift, p_s5_re, p_s5_im,
            s_mlstm_C, s_mlstm_n, s_mlstm_m, s_mlstm_conv, s_rwkv_S, s_rwkv_shift, s_s5_re, s_s5_im)
```

```python
import functools
import math

import jax
import jax.numpy as jnp
from jax import lax
from jax.experimental import pallas as pl
from jax.experimental.pallas import tpu as pltpu

F32 = jnp.float32
BF16 = jnp.bfloat16
HI = lax.Precision.HIGHEST

D_MODEL = 1024
DEPTH = 2
N_META = 16
HEAD_DIM = 64
D_A = 384
H_A = D_A // HEAD_DIM
CONV_W = 4
D_B = 384
H_B = D_B // HEAD_DIM
R_W = 64
R_A = 64
R_G = 128
RWKV_GN_EPS = 64e-5
D_C = 256
C_GROUP = 16
G_C = D_C // C_GROUP
P_C = 64
D_MIX = D_A + D_B + D_C
N_A_COLS = 4 * D_A + 2 * H_A
N_B_COLS = 3 * D_B + R_W + R_A + R_G
N_GROUPS = 4
EXP_PER_GROUP = 8
N_EXP = N_GROUPS * EXP_PER_GROUP
TOP_K = 2
D_EXPERT = 512
MOE_BLOCK = 256
DEEP_ALPHA = (2 * DEPTH) ** 0.25
LN_EPS = 1e-5

LANES = 128
SUBLANES = 8
VMEM_LIMIT = 48 * 1024 * 1024

ZA_W = 4 * D_A
ZB_W = 3 * D_B + 3 * LANES
Z_A0 = 0
Z_B0 = ZA_W
Z_GI0 = Z_B0 + ZB_W
Z_GF0 = Z_GI0 + LANES
Z_C0 = Z_GF0 + LANES
Z_N = Z_C0 + D_C
SEQ_PAD_MULT = 256
CHUNK = 64
TM = 256
NEG = -1e30


def _nt(a, b, precision=None):
    return lax.dot_general(a, b, (((1,), (1,)), ((), ())), precision=precision, preferred_element_type=F32)


def _tn(a, b, precision=None):
    return lax.dot_general(a, b, (((0,), (0,)), ((), ())), precision=precision, preferred_element_type=F32)


def _mm(a, b, precision=None):
    return jnp.dot(a, b, precision=precision, preferred_element_type=F32)


def _sigmoid(x):
    return 1.0 / (1.0 + jnp.exp(-x))


def _softplus(x):
    return jnp.maximum(x, 0.0) + jnp.log1p(jnp.exp(-jnp.abs(x)))


def _head_indicator(n, scale):
    shift = jnp.int32(int(math.log2(HEAD_DIM)))
    r = lax.shift_right_logical(lax.broadcasted_iota(jnp.int32, (n, n), 0), shift)
    c = lax.shift_right_logical(lax.broadcasted_iota(jnp.int32, (n, n), 1), shift)
    return jnp.where(r == c, scale, 0.0).astype(F32)


def _tril(n, strict=False):
    r = lax.broadcasted_iota(jnp.int32, (n, n), 0)
    c = lax.broadcasted_iota(jnp.int32, (n, n), 1)
    return (r > c) if strict else (r >= c)


def _in_proj_kernel(x_ref, w_ref, z_ref, *, tm, lp, npad):
    z = _mm(x_ref[...].astype(BF16), w_ref[...])
    if npad:
        pos = (pl.program_id(0) * tm) % lp + lax.broadcasted_iota(jnp.int32, (tm, 1), 0)
        z = jnp.where(pos >= npad, z, 0.0)
    z_ref[...] = z


def in_proj(x2d, w_bf16, lp, npad):
    t = x2d.shape[0]
    tm = min(TM, t)
    assert t % tm == 0 and (npad == 0 or lp % tm == 0)
    return pl.pallas_call(
        functools.partial(_in_proj_kernel, tm=tm, lp=lp, npad=npad),
        out_shape=jax.ShapeDtypeStruct((t, Z_N), F32),
        grid=(t // tm,),
        in_specs=[pl.BlockSpec((tm, D_MODEL), lambda i: (i, 0)),
                  pl.BlockSpec((D_MODEL, Z_N), lambda i: (0, 0))],
        out_specs=pl.BlockSpec((tm, Z_N), lambda i: (i, 0)),
        compiler_params=pltpu.CompilerParams(dimension_semantics=("arbitrary",), vmem_limit_bytes=VMEM_LIMIT),
        name="in_proj",
    )(x2d, w_bf16)


def _mlstm_kernel(zq_ref, zi_ref, zf_ref, c0_ref, n0_ref, m0_ref, cv0_ref, cw_ref, bi_ref, bf_ref, ng_ref,
                  y_ref, c_ref, n_ref, m_ref, cv_ref, cbuf, hbuf, *, lc, npad):
    c = pl.program_id(1)

    @pl.when(c == 0)
    def _():
        c_ref[...] = c0_ref[...]
        n_ref[...] = n0_ref[...]
        m_ref[...] = m0_ref[...]
        cbuf[0:SUBLANES, :] = cv0_ref[...]

    zq = zq_ref[...]
    cbuf[SUBLANES:, :] = zq[:, :2 * D_A]
    acc = zq[:, :2 * D_A] * cw_ref[CONV_W - 1:CONV_W, :]
    for j in range(CONV_W - 1):
        d = CONV_W - 1 - j
        acc = acc + cbuf[SUBLANES - d:SUBLANES - d + lc, :] * cw_ref[j:j + 1, :]
    qk = acc * _sigmoid(acc)
    last_rows = cbuf[lc:lc + SUBLANES, :]
    cbuf[0:SUBLANES, :] = last_rows
    cv_ref[...] = last_rows

    logi = zi_ref[...] + bi_ref[...]
    logf = -_softplus(-(zf_ref[...] + bf_ref[...]))
    if npad:
        pos = c * lc + lax.broadcasted_iota(jnp.int32, (lc, 1), 0)
        logi = jnp.where(pos >= npad, logi, NEG)
        logf = jnp.where(pos >= npad, logf, 0.0)
    causal = _tril(lc)
    bcum = _mm(causal.astype(F32), logf, HI)
    gcol = logi - bcum
    eye8 = (lax.broadcasted_iota(jnp.int32, (SUBLANES, LANES), 0)
            == lax.broadcasted_iota(jnp.int32, (SUBLANES, LANES), 1)).astype(F32)
    grow_all = _nt(eye8, gcol, HI)
    lane = lax.broadcasted_iota(jnp.int32, (1, LANES), 1)
    sub = lax.broadcasted_iota(jnp.int32, (SUBLANES, 1), 0)
    m_row = m_ref[...]
    n_all = n_ref[...]
    m_new = m_row
    n_new = n_all
    for h in range(H_A):
        sl = slice(h * HEAD_DIM, (h + 1) * HEAD_DIM)
        q = qk[:, sl]
        k = qk[:, D_A + h * HEAD_DIM:D_A + (h + 1) * HEAD_DIM] * (HEAD_DIM ** -0.5)
        v = zq[:, 2 * D_A + h * HEAD_DIM:2 * D_A + (h + 1) * HEAD_DIM]
        bcol = bcum[:, h:h + 1]
        dm = jnp.where(causal, bcol + grow_all[h:h + 1, :], NEG)
        g = bcol + m_row[:, h:h + 1]
        m = jnp.maximum(g, jnp.max(dm, axis=-1, keepdims=True))
        w_inter = jnp.exp(g - m)
        s = _nt(q, k, HI) * jnp.exp(dm - m)
        c0 = c_ref[h]
        n0 = n_all[h:h + 1, :]
        num = w_inter * _nt(q, c0, HI) + _mm(s, v, HI)
        den = w_inter * jnp.sum(q * n0, axis=-1, keepdims=True) + jnp.sum(s, axis=-1, keepdims=True)
        hbuf[:, sl] = num / jnp.maximum(jnp.abs(den), jnp.exp(-m))
        m_end = m[lc - 1:lc, :]
        w_end = jnp.exp(bcum[lc - 1:lc, h:h + 1] + gcol[:, h:h + 1] - m_end)
        dec = jnp.exp(g[lc - 1:lc, :] - m_end)
        c_ref[h] = dec * c0 + _tn(v * w_end, k, HI)
        n1 = dec * n0 + jnp.sum(k * w_end, axis=0, keepdims=True)
        m_new = jnp.where(lane == h, m_end, m_new)
        n_new = jnp.where(sub == h, n1, n_new)
    m_ref[...] = m_new
    n_ref[...] = n_new
    hh = hbuf[...]
    ind = _head_indicator(D_A, 1.0 / HEAD_DIM)
    mu = _mm(hh, ind, HI)
    dev = hh - mu
    var = _mm(dev * dev, ind, HI)
    o = _sigmoid(zq[:, 3 * D_A:4 * D_A])
    y_ref[...] = (o * (dev * lax.rsqrt(var + LN_EPS) * ng_ref[...])).astype(y_ref.dtype)


def mlstm_mixer(z3, c0, n0, m0, cv0, conv_w, gate_b, norm_g, lc, npad):
    b, lp, _ = z3.shape
    nc = lp // lc
    n0p = jnp.zeros((b, SUBLANES, HEAD_DIM), F32).at[:, :H_A].set(n0)
    m0p = jnp.zeros((b, 1, LANES), F32).at[:, 0, :H_A].set(m0)
    cv0p = jnp.zeros((b, SUBLANES, 2 * D_A), F32).at[:, SUBLANES - (CONV_W - 1):].set(cv0)
    cwp = jnp.zeros((SUBLANES, 2 * D_A), F32).at[:CONV_W].set(conv_w)
    bi = jnp.zeros((1, LANES), F32).at[0, :H_A].set(gate_b[0])
    bf = jnp.zeros((1, LANES), F32).at[0, :H_A].set(gate_b[1])
    st = lambda *shape: pl.BlockSpec((None,) + shape, lambda i, j: (i,) + (0,) * len(shape))
    par = lambda *shape: pl.BlockSpec(shape, lambda i, j: (0,) * len(shape))
    y, c1, n1, m1, cv1 = pl.pallas_call(
        functools.partial(_mlstm_kernel, lc=lc, npad=npad),
        out_shape=(jax.ShapeDtypeStruct((b, lp, D_A), BF16),
                   jax.ShapeDtypeStruct((b, H_A, HEAD_DIM, HEAD_DIM), F32),
                   jax.ShapeDtypeStruct((b, SUBLANES, HEAD_DIM), F32),
                   jax.ShapeDtypeStruct((b, 1, LANES), F32),
                   jax.ShapeDtypeStruct((b, SUBLANES, 2 * D_A), F32)),
        grid=(b, nc),
        in_specs=[pl.BlockSpec((None, lc, ZA_W), lambda i, j: (i, j, Z_A0 // ZA_W)),
                  pl.BlockSpec((None, lc, LANES), lambda i, j: (i, j, Z_GI0 // LANES)),
                  pl.BlockSpec((None, lc, LANES), lambda i, j: (i, j, Z_GF0 // LANES)),
                  st(H_A, HEAD_DIM, HEAD_DIM), st(SUBLANES, HEAD_DIM), st(1, LANES), st(SUBLANES, 2 * D_A),
                  par(SUBLANES, 2 * D_A), par(1, LANES), par(1, LANES), par(1, D_A)],
        out_specs=(pl.BlockSpec((None, lc, D_A), lambda i, j: (i, j, 0)),
                   st(H_A, HEAD_DIM, HEAD_DIM), st(SUBLANES, HEAD_DIM), st(1, LANES), st(SUBLANES, 2 * D_A)),
        scratch_shapes=[pltpu.VMEM((lc + SUBLANES, 2 * D_A), F32), pltpu.VMEM((lc, D_A), F32)],
        compiler_params=pltpu.CompilerParams(dimension_semantics=("arbitrary", "arbitrary"),
                                             vmem_limit_bytes=VMEM_LIMIT),
        name="mlstm",
    )(z3, z3, z3, c0, n0p, m0p, cv0p, cwp, bi, bf, norm_g.reshape(1, D_A))
    return y, (c1, n1[:, :H_A], m1[:, 0, :H_A], cv1[:, SUBLANES - (CONV_W - 1):])


def _rwkv_kernel(zb_ref, s0_ref, sh0_ref, mu_ref, vec_ref, wup_ref, aup_ref, gup_ref,
                 y_ref, s_ref, sh_ref, ybuf, *, lc):
    c = pl.program_id(1)

    @pl.when(c == 0)
    def _():
        s_ref[...] = s0_ref[...]
        sh_ref[...] = sh0_ref[...]

    z = zb_ref[...]
    row = lax.broadcasted_iota(jnp.int32, (lc, 1), 0)
    zprev = jnp.where(row == 0, sh_ref[SUBLANES - 1:SUBLANES, :], pltpu.roll(z, 1, axis=0))
    zs = z + mu_ref[...] * (zprev - z)
    sh_ref[...] = z[lc - SUBLANES:lc, :]
    r = zs[:, 0:D_B]
    k = zs[:, D_B:2 * D_B]
    v = zs[:, 2 * D_B:3 * D_B]
    xw = zs[:, 3 * D_B:3 * D_B + LANES]
    xa = zs[:, 3 * D_B + LANES:3 * D_B + 2 * LANES]
    xg = zs[:, 3 * D_B + 2 * LANES:3 * D_B + 3 * LANES]
    w0, a0, k_k, k_a, r_k, gn_g = (vec_ref[i:i + 1, :] for i in range(6))
    w_log = -_softplus(-(w0 + _mm(jnp.tanh(xw), wup_ref[...], HI))) - 0.5
    lw = -jnp.exp(w_log)
    a = _sigmoid(a0 + _mm(xa, aup_ref[...], HI))
    g = _mm(_sigmoid(xg), gup_ref[...], HI)
    ones_h = _head_indicator(D_B, 1.0)
    kk = k * k_k
    kk = kk / jnp.maximum(jnp.sqrt(_mm(kk * kk, ones_h, HI)), 1e-12)
    k2 = k * (1.0 + (a - 1.0) * k_a)
    av = -kk
    bv = kk * a
    incl = _tril(lc)
    strict = _tril(lc, strict=True)
    cum = _mm(incl.astype(F32), lw, HI)
    c_last = cum[lc - 1:lc, :]
    e_neg = jnp.exp(-cum)
    e_end = jnp.exp(c_last - cum)
    rt = r * jnp.exp(cum)
    at = av * jnp.exp(cum - lw)
    bt = bv * e_neg
    kt = k2 * e_neg
    bh = bv * e_end
    kh = k2 * e_end
    p_last = jnp.exp(c_last)
    n_dbl = int(math.log2(lc)) - 1
    for h in range(H_B):
        sl = slice(h * HEAD_DIM, (h + 1) * HEAD_DIM)
        s0 = s_ref[h]
        vh = v[:, sl]
        a_ab = jnp.where(strict, _nt(at[:, sl], bt[:, sl], HI), 0.0)
        a_ak = jnp.where(strict, _nt(at[:, sl], kt[:, sl], HI), 0.0)
        b_rb = jnp.where(incl, _nt(rt[:, sl], bt[:, sl], HI), 0.0)
        b_rk = jnp.where(incl, _nt(rt[:, sl], kt[:, sl], HI), 0.0)
        w = _nt(at[:, sl], s0, HI) + _mm(a_ak, vh, HI)
        u = w + _mm(a_ab, w, HI)
        pw = a_ab
        for _ in range(n_dbl):
            pw = _mm(pw, pw, HI)
            u = u + _mm(pw, u, HI)
        ybuf[:, sl] = _nt(rt[:, sl], s0, HI) + _mm(b_rb, u, HI) + _mm(b_rk, vh, HI)
        s_ref[h] = s0 * p_last[:, sl] + _tn(u, bh[:, sl], HI) + _tn(vh, kh[:, sl], HI)
    y = ybuf[...]
    ind = _head_indicator(D_B, 1.0 / HEAD_DIM)
    mu = _mm(y, ind, HI)
    dev = y - mu
    var = _mm(dev * dev, ind, HI)
    bonus = _mm(r * k2 * r_k, ones_h, HI) * v
    y_ref[...] = ((dev * lax.rsqrt(var + RWKV_GN_EPS) * gn_g + bonus) * g).astype(y_ref.dtype)


def _pad_b_cols(t):
    o = 3 * D_B
    pad = jnp.zeros(t.shape[:-1] + (LANES - R_W,), t.dtype)
    return jnp.concatenate([t[..., :o + R_W], pad, t[..., o + R_W:o + R_W + R_A], pad, t[..., o + R_W + R_A:]], axis=-1)


def _unpad_b_cols(t):
    o = 3 * D_B
    return jnp.concatenate([t[..., :o + R_W], t[..., o + LANES:o + LANES + R_A], t[..., o + 2 * LANES:]], axis=-1)


def rwkv_mixer(z3, s0, sh0, mu, vecs, w_up, a_up, g_up, lc):
    b, lp, _ = z3.shape
    nc = lp // lc
    sh0p = jnp.zeros((b, SUBLANES, ZB_W), F32).at[:, SUBLANES - 1:].set(_pad_b_cols(sh0))
    mup = _pad_b_cols(mu.reshape(1, N_B_COLS))
    vecp = jnp.zeros((SUBLANES, D_B), F32).at[:6].set(vecs)
    wupp = jnp.zeros((LANES, D_B), F32).at[:R_W].set(w_up)
    aupp = jnp.zeros((LANES, D_B), F32).at[:R_A].set(a_up)
    st = lambda *shape: pl.BlockSpec((None,) + shape, lambda i, j: (i,) + (0,) * len(shape))
    par = lambda *shape: pl.BlockSpec(shape, lambda i, j: (0,) * len(shape))
    y, s1, sh1 = pl.pallas_call(
        functools.partial(_rwkv_kernel, lc=lc),
        out_shape=(jax.ShapeDtypeStruct((b, lp, D_B), BF16),
                   jax.ShapeDtypeStruct((b, H_B, HEAD_DIM, HEAD_DIM), F32),
                   jax.ShapeDtypeStruct((b, SUBLANES, ZB_W), F32)),
        grid=(b, nc),
        in_specs=[pl.BlockSpec((None, lc, ZB_W), lambda i, j: (i, j, Z_B0 // ZB_W)),
                  st(H_B, HEAD_DIM, HEAD_DIM), st(SUBLANES, ZB_W),
                  par(1, ZB_W), par(SUBLANES, D_B), par(LANES, D_B), par(LANES, D_B), par(R_G, D_B)],
        out_specs=(pl.BlockSpec((None, lc, D_B), lambda i, j: (i, j, 0)),
                   st(H_B, HEAD_DIM, HEAD_DIM), st(SUBLANES, ZB_W)),
        scratch_shapes=[pltpu.VMEM((lc, D_B), F32)],
        compiler_params=pltpu.CompilerParams(dimension_semantics=("arbitrary", "arbitrary"),
                                             vmem_limit_bytes=VMEM_LIMIT),
        name="rwkv",
    )(z3, s0, sh0p, mup, vecp, wupp, aupp, g_up)
    return y, (s1, _unpad_b_cols(sh1[:, SUBLANES - 1:]))


def _s5_kernel(u_ref, hr0_ref, hi0_ref, are_ref, aim_ref, ldt_ref, bre_ref, bim_ref, cre_ref, cim_ref,
               d_ref, wglu_ref, y_ref, hr_ref, hi_ref, bbr, bbi, *, lc):
    c = pl.program_id(1)
    a_re = are_ref[...]
    a_im = aim_ref[...]
    dt = jnp.exp(ldt_ref[...])
    mag = jnp.exp(a_re * dt)
    ab_re = mag * jnp.cos(a_im * dt)
    ab_im = mag * jnp.sin(a_im * dt)

    @pl.when(c == 0)
    def _():
        hr_ref[...] = hr0_ref[...]
        hi_ref[...] = hi0_ref[...]
        inv = 1.0 / (a_re * a_re + a_im * a_im)
        q_re = ((ab_re - 1.0) * a_re + ab_im * a_im) * inv
        q_im = (ab_im * a_re - (ab_re - 1.0) * a_im) * inv
        bbr[...] = q_re * bre_ref[...] - q_im * bim_ref[...]
        bbi[...] = q_re * bim_ref[...] + q_im * bre_ref[...]

    u = u_ref[...]
    row = lax.broadcasted_iota(jnp.int32, (lc, 1), 0)
    h_re = hr_ref[...]
    h_im = hi_ref[...]
    first = row == 0
    x_re = _mm(u, bbr[...], HI) + jnp.where(first, ab_re * h_re - ab_im * h_im, 0.0)
    x_im = _mm(u, bbi[...], HI) + jnp.where(first, ab_re * h_im + ab_im * h_re, 0.0)
    p_re, p_im = ab_re, ab_im
    shift = 1
    while shift < lc:
        keep = row >= shift
        s_re = jnp.where(keep, pltpu.roll(x_re, shift, axis=0), 0.0)
        s_im = jnp.where(keep, pltpu.roll(x_im, shift, axis=0), 0.0)
        x_re, x_im = x_re + p_re * s_re - p_im * s_im, x_im + p_re * s_im + p_im * s_re
        p_re, p_im = p_re * p_re - p_im * p_im, 2.0 * p_re * p_im
        shift *= 2
    hr_ref[...] = x_re[lc - 1:lc, :]
    hi_ref[...] = x_im[lc - 1:lc, :]
    y = _mm(x_re, cre_ref[...], HI) - _mm(x_im, cim_ref[...], HI) + d_ref[...] * u
    y = 0.5 * y * (1.0 + jnp.tanh(math.sqrt(2.0 / math.pi) * (y + 0.044715 * (y * y * y))))
    y_ref[...] = (y * _sigmoid(_mm(y, wglu_ref[...], HI))).astype(y_ref.dtype)


def _block_diag(t):
    g, r, c = t.shape
    eye = jnp.eye(g, dtype=t.dtype)
    return (t[:, :, None, :] * eye[:, None, :, None]).reshape(g * r, g * c)


def s5_mixer(z3, hr0, hi0, a_re, a_im, log_dt, b_re, b_im, c_re, c_im, d_skip, w_glu, lc):
    b, lp, _ = z3.shape
    nc = lp // lc
    gp = G_C * P_C
    flat = lambda t: t.reshape(1, gp).astype(F32)
    ldt = jnp.broadcast_to(log_dt.astype(F32)[:, None], (G_C, P_C)).reshape(1, gp)
    bre = _block_diag(jnp.swapaxes(b_re, 1, 2))
    bim = _block_diag(jnp.swapaxes(b_im, 1, 2))
    cre = _block_diag(jnp.swapaxes(c_re, 1, 2))
    cim = _block_diag(jnp.swapaxes(c_im, 1, 2))
    st = lambda *shape: pl.BlockSpec((None,) + shape, lambda i, j: (i,) + (0,) * len(shape))
    par = lambda *shape: pl.BlockSpec(shape, lambda i, j: (0,) * len(shape))
    y, hr1, hi1 = pl.pallas_call(
        functools.partial(_s5_kernel, lc=lc),
        out_shape=(jax.ShapeDtypeStruct((b, lp, D_C), BF16),
                   jax.ShapeDtypeStruct((b, 1, gp), F32),
                   jax.ShapeDtypeStruct((b, 1, gp), F32)),
        grid=(b, nc),
        in_specs=[pl.BlockSpec((None, lc, D_C), lambda i, j: (i, j, Z_C0 // D_C)),
                  st(1, gp), st(1, gp), par(1, gp), par(1, gp), par(1, gp),
                  par(D_C, gp), par(D_C, gp), par(gp, D_C), par(gp, D_C), par(1, D_C), par(D_C, D_C)],
        out_specs=(pl.BlockSpec((None, lc, D_C), lambda i, j: (i, j, 0)), st(1, gp), st(1, gp)),
        scratch_shapes=[pltpu.VMEM((D_C, gp), F32), pltpu.VMEM((D_C, gp), F32)],
        compiler_params=pltpu.CompilerParams(dimension_semantics=("arbitrary", "arbitrary"),
                                             vmem_limit_bytes=VMEM_LIMIT),
        name="s5",
    )(z3, hr0.reshape(b, 1, gp), hi0.reshape(b, 1, gp), flat(a_re), flat(a_im), ldt,
      bre, bim, cre, cim, d_skip.reshape(1, D_C), w_glu)
    return y, (hr1.reshape(b, G_C, P_C), hi1.reshape(b, G_C, P_C))


def _layer_norm(x, g, b):
    mu = jnp.mean(x, axis=-1, keepdims=True)
    dev = x - mu
    var = jnp.mean(dev * dev, axis=-1, keepdims=True)
    return dev * lax.rsqrt(var + LN_EPS) * g + b


def _argmax_lane(x, lane):
    m = jnp.max(x, axis=-1, keepdims=True)
    idx = jnp.min(jnp.where(x == m, lane, LANES), axis=-1, keepdims=True)
    return m, idx


def _out_proj_kernel(x_ref, ya_ref, yb_ref, yc_ref, wa_ref, wb_ref, wc_ref, g_ref, b_ref,
                     wco_ref, bco_ref, wfi_ref, bfi_ref, x1_ref, rt_ref):
    mix = _mm(ya_ref[...], wa_ref[...]) + _mm(yb_ref[...], wb_ref[...]) + _mm(yc_ref[...], wc_ref[...])
    x1 = _layer_norm(DEEP_ALPHA * x_ref[...] + mix, g_ref[...], b_ref[...])
    x1_ref[...] = x1
    tm = x1.shape[0]
    lane = lax.broadcasted_iota(jnp.int32, (tm, LANES), 1)
    lco = jnp.where(lane < N_GROUPS, _mm(x1, wco_ref[...], HI) + bco_ref[...], NEG)
    mco, grp = _argmax_lane(lco, lane)
    p_grp = 1.0 / jnp.sum(jnp.exp(lco - mco), axis=-1, keepdims=True)
    in_grp = (lane >= grp * EXP_PER_GROUP) & (lane < (grp + 1) * EXP_PER_GROUP)
    lfi = jnp.where(in_grp, _mm(x1, wfi_ref[...], HI) + bfi_ref[...], NEG)
    m0, e0 = _argmax_lane(lfi, lane)
    m1, e1 = _argmax_lane(jnp.where(lane == e0, NEG, lfi), lane)
    t1 = jnp.exp(m1 - m0)
    g0 = p_grp / (1.0 + t1)
    g1 = p_grp * t1 / (1.0 + t1)
    rt_ref[...] = jnp.where(lane == 0, e0.astype(F32),
                            jnp.where(lane == 1, e1.astype(F32),
                                      jnp.where(lane == 2, g0, jnp.where(lane == 3, g1, 0.0))))


def out_proj_router(x2d, ya, yb, yc, w_out_bf16, ln_g, ln_b, w_coarse, b_coarse, w_fine, b_fine):
    t = x2d.shape[0]
    tm = min(TM, t)
    wco = jnp.zeros((D_MODEL, LANES), F32).at[:, :N_GROUPS].set(w_coarse)
    bco = jnp.zeros((1, LANES), F32).at[0, :N_GROUPS].set(b_coarse)
    wfi = jnp.zeros((D_MODEL, LANES), F32).at[:, :N_EXP].set(w_fine)
    bfi = jnp.zeros((1, LANES), F32).at[0, :N_EXP].set(b_fine)
    tok = lambda w: pl.BlockSpec((tm, w), lambda i: (i, 0))
    par = lambda *shape: pl.BlockSpec(shape, lambda i: (0,) * len(shape))
    return pl.pallas_call(
        _out_proj_kernel,
        out_shape=(jax.ShapeDtypeStruct((t, D_MODEL), F32), jax.ShapeDtypeStruct((t, LANES), F32)),
        grid=(t // tm,),
        in_specs=[tok(D_MODEL), tok(D_A), tok(D_B), tok(D_C),
                  par(D_A, D_MODEL), par(D_B, D_MODEL), par(D_C, D_MODEL), par(1, D_MODEL), par(1, D_MODEL),
                  par(D_MODEL, LANES), par(1, LANES), par(D_MODEL, LANES), par(1, LANES)],
        out_specs=(tok(D_MODEL), tok(LANES)),
        compiler_params=pltpu.CompilerParams(dimension_semantics=("arbitrary",), vmem_limit_bytes=VMEM_LIMIT),
        name="out_proj_router",
    )(x2d, ya, yb, yc, w_out_bf16[:D_A], w_out_bf16[D_A:D_A + D_B], w_out_bf16[D_A + D_B:],
      ln_g.reshape(1, D_MODEL), ln_b.reshape(1, D_MODEL), wco, bco, wfi, bfi)


def _row_copy(src_hbm, dst, sem, tok, r):
    return pltpu.make_async_copy(src_hbm.at[pl.ds(tok, 1)], dst.at[pl.ds(r, 1)], sem)


def _ffn_kernel(be_ref, rt_ref, x_hbm, wg_ref, wu_ref, wd_ref, ys_ref, xbuf, sem):
    i = pl.program_id(0)
    n = pl.num_programs(0)
    slot = i % 2

    def gather(block, slot_):
        def body(r, carry):
            _row_copy(x_hbm, xbuf.at[slot_], sem.at[slot_], rt_ref[block * MOE_BLOCK + r], r).start()
            return carry
        lax.fori_loop(0, MOE_BLOCK, body, 0)

    @pl.when(i == 0)
    def _():
        gather(0, 0)

    @pl.when(i + 1 < n)
    def _():
        gather(i + 1, 1 - slot)

    def wait_body(r, carry):
        _row_copy(x_hbm, xbuf.at[slot], sem.at[slot], 0, r).wait()
        return carry
    lax.fori_loop(0, MOE_BLOCK, wait_body, 0)

    xb = xbuf[slot].astype(BF16)
    hg = _mm(xb, wg_ref[...])
    hu = _mm(xb, wu_ref[...])
    h = (hg * _sigmoid(hg) * hu).astype(BF16)
    ys_ref[...] = _mm(h, wd_ref[...])


def expert_ffn(x1, block_e, row_tok, wg, wu, wd):
    n_rows = row_tok.shape[0]
    n_blocks = n_rows // MOE_BLOCK
    grid_spec = pltpu.PrefetchScalarGridSpec(
        num_scalar_prefetch=2,
        grid=(n_blocks,),
        in_specs=[pl.BlockSpec(memory_space=pl.ANY),
                  pl.BlockSpec((None, D_MODEL, D_EXPERT), lambda i, be, rt: (be[i], 0, 0)),
                  pl.BlockSpec((None, D_MODEL, D_EXPERT), lambda i, be, rt: (be[i], 0, 0)),
                  pl.BlockSpec((None, D_EXPERT, D_MODEL), lambda i, be, rt: (be[i], 0, 0))],
        out_specs=pl.BlockSpec((MOE_BLOCK, D_MODEL), lambda i, be, rt: (i, 0)),
        scratch_shapes=[pltpu.VMEM((2, MOE_BLOCK, D_MODEL), F32), pltpu.SemaphoreType.DMA((2,))],
    )
    return pl.pallas_call(
        _ffn_kernel,
        out_shape=jax.ShapeDtypeStruct((n_rows, D_MODEL), F32),
        grid_spec=grid_spec,
        compiler_params=pltpu.CompilerParams(dimension_semantics=("arbitrary",), vmem_limit_bytes=VMEM_LIMIT),
        name="expert_ffn",
    )(block_e, row_tok, x1, wg, wu, wd)


def _combine_kernel(d_ref, x1_ref, rt_ref, g_ref, b_ref, ys_hbm, o_ref, buf, sem, *, tm):
    i = pl.program_id(0)
    n = pl.num_programs(0)
    slot = i % 2

    def gather(tile, slot_):
        def body(r, carry):
            base = (tile * tm + r) * TOP_K
            _row_copy(ys_hbm, buf.at[slot_, 0], sem.at[slot_], d_ref[base], r).start()
            _row_copy(ys_hbm, buf.at[slot_, 1], sem.at[slot_], d_ref[base + 1], r).start()
            return carry
        lax.fori_loop(0, tm, body, 0)

    @pl.when(i == 0)
    def _():
        gather(0, 0)

    @pl.when(i + 1 < n)
    def _():
        gather(i + 1, 1 - slot)

    def wait_body(r, carry):
        _row_copy(ys_hbm, buf.at[slot, 0], sem.at[slot], 0, r).wait()
        _row_copy(ys_hbm, buf.at[slot, 1], sem.at[slot], 0, r).wait()
        return carry
    lax.fori_loop(0, tm, wait_body, 0)

    rt = rt_ref[...]
    ffn = rt[:, 2:3] * buf[slot, 0] + rt[:, 3:4] * buf[slot, 1]
    o_ref[...] = _layer_norm(DEEP_ALPHA * x1_ref[...] + ffn, g_ref[...], b_ref[...])


def combine_ln(x1, route, dest, ys, ln_g, ln_b):
    t = x1.shape[0]
    tm = min(TM, t)
    grid_spec = pltpu.PrefetchScalarGridSpec(
        num_scalar_prefetch=1,
        grid=(t // tm,),
        in_specs=[pl.BlockSpec((tm, D_MODEL), lambda i, d: (i, 0)),
                  pl.BlockSpec((tm, LANES), lambda i, d: (i, 0)),
                  pl.BlockSpec((1, D_MODEL), lambda i, d: (0, 0)),
                  pl.BlockSpec((1, D_MODEL), lambda i, d: (0, 0)),
                  pl.BlockSpec(memory_space=pl.ANY)],
        out_specs=pl.BlockSpec((tm, D_MODEL), lambda i, d: (i, 0)),
        scratch_shapes=[pltpu.VMEM((2, TOP_K, tm, D_MODEL), F32), pltpu.SemaphoreType.DMA((2,))],
    )
    return pl.pallas_call(
        functools.partial(_combine_kernel, tm=tm),
        out_shape=jax.ShapeDtypeStruct((t, D_MODEL), F32),
        grid_spec=grid_spec,
        compiler_params=pltpu.CompilerParams(dimension_semantics=("arbitrary",), vmem_limit_bytes=VMEM_LIMIT),
        name="combine_ln",
    )(dest, x1, route, ln_g.reshape(1, D_MODEL), ln_b.reshape(1, D_MODEL), ys)


def route_tables(route):
    t = route.shape[0]
    a = t * TOP_K
    n_blocks = -(-a // MOE_BLOCK) + N_EXP
    flat_e = route[:, :TOP_K].astype(jnp.int32).reshape(-1)
    onehot = (flat_e[:, None] == jnp.arange(N_EXP, dtype=jnp.int32)[None, :]).astype(jnp.int32)
    csum = jnp.cumsum(onehot, axis=0)
    counts = csum[-1]
    rank = jnp.take_along_axis(csum, flat_e[:, None], axis=1)[:, 0] - 1
    padded = (counts + MOE_BLOCK - 1) // MOE_BLOCK * MOE_BLOCK
    pad_end = jnp.cumsum(padded)
    pad_start = pad_end - padded
    dest = (pad_start[flat_e] + rank).astype(jnp.int32)
    row_tok = jnp.zeros((n_blocks * MOE_BLOCK,), jnp.int32).at[dest].set(jnp.arange(a, dtype=jnp.int32) // TOP_K)
    block_e = jnp.minimum(jnp.searchsorted(pad_end, jnp.arange(n_blocks, dtype=jnp.int32) * MOE_BLOCK, side='right'),
                          N_EXP - 1).astype(jnp.int32)
    return block_e, row_tok, dest


def _prep_w_in(w_in):
    wa = w_in[:, :ZA_W]
    gi = w_in[:, ZA_W:ZA_W + H_A]
    gf = w_in[:, ZA_W + H_A:N_A_COLS]
    wb = _pad_b_cols(w_in[:, N_A_COLS:N_A_COLS + N_B_COLS])
    wc = w_in[:, N_A_COLS + N_B_COLS:]
    gpad = jnp.zeros((D_MODEL, LANES - H_A), w_in.dtype)
    return jnp.concatenate([wa, wb, gi, gpad, gf, gpad, wc], axis=1).astype(BF16)


def _trunk_layer(x3, st, npad, lc, p):
    b, lp, _ = x3.shape
    c0, n0, m0, cv0, s0, sh0, hr0, hi0 = st
    x2d = x3.reshape(b * lp, D_MODEL)
    z3 = in_proj(x2d, p['w_in'], lp, npad).reshape(b, lp, Z_N)
    ya, st_a = mlstm_mixer(z3, c0, n0, m0, cv0, p['mlstm_conv_w'], p['mlstm_gate_b'], p['mlstm_norm_g'], lc, npad)
    yb, st_b = rwkv_mixer(z3, s0, sh0, p['rwkv_mu'], p['rwkv_vecs'], p['rwkv_w_up'], p['rwkv_a_up'], p['rwkv_g_up'], lc)
    yc, st_c = s5_mixer(z3, hr0, hi0, p['s5_A_re'], p['s5_A_im'], p['s5_log_dt'], p['s5_B_re'], p['s5_B_im'],
                        p['s5_C_re'], p['s5_C_im'], p['s5_D'], p['s5_w_glu'], lc)
    t = b * lp
    x1, route = out_proj_router(x2d, ya.reshape(t, D_A), yb.reshape(t, D_B), yc.reshape(t, D_C), p['w_out'],
                                p['ln_g'][0], p['ln_b'][0], p['moe_w_coarse'], p['moe_b_coarse'],
                                p['moe_w_fine'], p['moe_b_fine'])
    block_e, row_tok, dest = route_tables(route)
    ys = expert_ffn(x1, block_e, row_tok, p['moe_w_gate'], p['moe_w_up'], p['moe_w_down'])
    x2 = combine_ln(x1, route, dest, ys, p['ln_g'][1], p['ln_b'][1])
    return x2.reshape(b, lp, D_MODEL), st_a + st_b + st_c


def _run_trunk(x3, states, npad, lc, params):
    new = [[] for _ in states]
    for l in range(DEPTH):
        p = {name: arr[l] for name, arr in params.items()}
        x3, st_new = _trunk_layer(x3, [s[l].astype(F32) for s in states], npad, lc, p)
        for lst, s in zip(new, st_new):
            lst.append(s)
    return x3, [jnp.stack(lst) for lst in new]


def kernel(x_prompt, x_sample, state_mlstm_C, state_mlstm_n, state_mlstm_m, state_mlstm_conv, state_rwkv_S, state_rwkv_shift, state_s5_re, state_s5_im, meta_tokens, w_in, w_out, mlstm_conv_w, mlstm_gate_b, mlstm_norm_g, rwkv_mu, rwkv_vecs, rwkv_w_up, rwkv_a_up, rwkv_g_up, s5_A_re, s5_A_im, s5_log_dt, s5_B_re, s5_B_im, s5_C_re, s5_C_im, s5_D, s5_w_glu, ln_g, ln_b, moe_w_coarse, moe_b_coarse, moe_w_fine, moe_b_fine, moe_w_gate, moe_w_up, moe_w_down):
    params = dict(w_in=jax.vmap(_prep_w_in)(w_in), w_out=w_out.astype(BF16), mlstm_conv_w=mlstm_conv_w,
                  mlstm_gate_b=mlstm_gate_b, mlstm_norm_g=mlstm_norm_g, rwkv_mu=rwkv_mu, rwkv_vecs=rwkv_vecs,
                  rwkv_w_up=rwkv_w_up, rwkv_a_up=rwkv_a_up, rwkv_g_up=rwkv_g_up, s5_A_re=s5_A_re, s5_A_im=s5_A_im,
                  s5_log_dt=s5_log_dt, s5_B_re=s5_B_re, s5_B_im=s5_B_im, s5_C_re=s5_C_re, s5_C_im=s5_C_im,
                  s5_D=s5_D, s5_w_glu=s5_w_glu, ln_g=ln_g, ln_b=ln_b, moe_w_coarse=moe_w_coarse,
                  moe_b_coarse=moe_b_coarse, moe_w_fine=moe_w_fine, moe_b_fine=moe_b_fine,
                  moe_w_gate=moe_w_gate.astype(BF16), moe_w_up=moe_w_up.astype(BF16),
                  moe_w_down=moe_w_down.astype(BF16))
    state_in = [state_mlstm_C, state_mlstm_n, state_mlstm_m, state_mlstm_conv,
                state_rwkv_S, state_rwkv_shift, state_s5_re, state_s5_im]

    b, seq, _ = x_prompt.shape
    real = N_META + seq
    npad = (-real) % SEQ_PAD_MULT
    meta = jnp.broadcast_to(meta_tokens.astype(x_prompt.dtype)[None], (b, N_META, D_MODEL))
    xp = jnp.concatenate([jnp.zeros((b, npad, D_MODEL), x_prompt.dtype), meta, x_prompt], axis=1)
    fresh = [jnp.zeros((DEPTH, b) + s.shape[2:], F32) for s in state_in]
    yp, pst = _run_trunk(xp, fresh, npad, CHUNK, params)
    y_prompt = yp[:, npad + N_META:]
    pst = [s.astype(r.dtype) for s, r in zip(pst, state_in)]

    y_sample, sst = _run_trunk(x_sample, state_in, 0, CHUNK, params)
    sst = [s.astype(r.dtype) for s, r in zip(sst, state_in)]
    return (y_prompt, y_sample, *pst, *sst)
```

```python
import functools
import math

import jax
import jax.numpy as jnp
from jax import lax
from jax.experimental import pallas as pl
from jax.experimental.pallas import tpu as pltpu

F32 = jnp.float32
BF16 = jnp.bfloat16
HI = lax.Precision.HIGHEST

D_MODEL = 1024
DEPTH = 2
N_META = 16
HEAD_DIM = 64
D_A = 384
H_A = D_A // HEAD_DIM
CONV_W = 4
D_B = 384
H_B = D_B // HEAD_DIM
R_W = 64
R_A = 64
R_G = 128
RWKV_GN_EPS = 64e-5
D_C = 256
C_GROUP = 16
G_C = D_C // C_GROUP
P_C = 64
D_MIX = D_A + D_B + D_C
N_A_COLS = 4 * D_A + 2 * H_A
N_B_COLS = 3 * D_B + R_W + R_A + R_G
N_GROUPS = 4
EXP_PER_GROUP = 8
N_EXP = N_GROUPS * EXP_PER_GROUP
TOP_K = 2
D_EXPERT = 512
MOE_BLOCK = 256
DEEP_ALPHA = (2 * DEPTH) ** 0.25
LN_EPS = 1e-5

LANES = 128
SUBLANES = 8
VMEM_LIMIT = 48 * 1024 * 1024

ZA_W = 4 * D_A
ZB_W = 3 * D_B + 3 * LANES
Z_A0 = 0
Z_B0 = ZA_W
Z_GI0 = Z_B0 + ZB_W
Z_GF0 = Z_GI0 + LANES
Z_C0 = Z_GF0 + LANES
Z_N = Z_C0 + D_C
SEQ_PAD_MULT = 256
CHUNK = 64
CHUNK_MLSTM = 128
BATCH_PER_STEP = 2
TM = 256
NEG = -1e30


def _nt(a, b, precision=None):
    return lax.dot_general(a, b, (((1,), (1,)), ((), ())), precision=precision, preferred_element_type=F32)


def _tn(a, b, precision=None):
    return lax.dot_general(a, b, (((0,), (0,)), ((), ())), precision=precision, preferred_element_type=F32)


def _mm(a, b, precision=None):
    return jnp.dot(a, b, precision=precision, preferred_element_type=F32)


def _bf(x):
    return x.astype(BF16)


def _cumsum_rows(tril_bf16, x):
    hi = _bf(x)
    lo = _bf(x - hi.astype(F32))
    return _mm(tril_bf16, hi) + _mm(tril_bf16, lo)


def _sigmoid(x):
    return 1.0 / (1.0 + jnp.exp(-x))


def _softplus(x):
    return jnp.maximum(x, 0.0) + jnp.log1p(jnp.exp(-jnp.abs(x)))


def _head_indicator(n, scale):
    shift = jnp.int32(int(math.log2(HEAD_DIM)))
    r = lax.shift_right_logical(lax.broadcasted_iota(jnp.int32, (n, n), 0), shift)
    c = lax.shift_right_logical(lax.broadcasted_iota(jnp.int32, (n, n), 1), shift)
    return jnp.where(r == c, scale, 0.0).astype(F32)


def _tril(n, strict=False):
    r = lax.broadcasted_iota(jnp.int32, (n, n), 0)
    c = lax.broadcasted_iota(jnp.int32, (n, n), 1)
    return (r > c) if strict else (r >= c)


def _in_proj_kernel(x_ref, w_ref, z_ref, *, tm, lp, npad):
    z = _mm(x_ref[...].astype(BF16), w_ref[...])
    if npad:
        pos = (pl.program_id(0) * tm) % lp + lax.broadcasted_iota(jnp.int32, (tm, 1), 0)
        z = jnp.where(pos >= npad, z, 0.0)
    z_ref[...] = z


def in_proj(x2d, w_bf16, lp, npad):
    t = x2d.shape[0]
    tm = min(TM, t)
    assert t % tm == 0 and (npad == 0 or lp % tm == 0)
    return pl.pallas_call(
        functools.partial(_in_proj_kernel, tm=tm, lp=lp, npad=npad),
        out_shape=jax.ShapeDtypeStruct((t, Z_N), F32),
        grid=(t // tm,),
        in_specs=[pl.BlockSpec((tm, D_MODEL), lambda i: (i, 0)),
                  pl.BlockSpec((D_MODEL, Z_N), lambda i: (0, 0))],
        out_specs=pl.BlockSpec((tm, Z_N), lambda i: (i, 0)),
        compiler_params=pltpu.CompilerParams(dimension_semantics=("arbitrary",), vmem_limit_bytes=VMEM_LIMIT),
        name="in_proj",
    )(x2d, w_bf16)


def _mlstm_kernel(zq_ref, zi_ref, zf_ref, c0_ref, n0_ref, m0_ref, cv0_ref, cw_ref, bi_ref, bf_ref, ng_ref,
                  y_ref, c_ref, n_ref, m_ref, cv_ref, cbuf, *, lc, npad, bb):
    c = pl.program_id(1)

    @pl.when(c == 0)
    def _():
        c_ref[...] = c0_ref[...]
        n_ref[...] = n0_ref[...]
        m_ref[...] = m0_ref[...]
        cbuf[:, 0:SUBLANES, :] = cv0_ref[...]

    causal = _tril(lc)
    tril_b = causal.astype(BF16)
    eye8 = (lax.broadcasted_iota(jnp.int32, (SUBLANES, LANES), 0)
            == lax.broadcasted_iota(jnp.int32, (SUBLANES, LANES), 1)).astype(BF16)
    lane = lax.broadcasted_iota(jnp.int32, (1, LANES), 1)
    sub = lax.broadcasted_iota(jnp.int32, (SUBLANES, 1), 0)
    ind = _head_indicator(D_A, 1.0 / HEAD_DIM).astype(BF16)
    prep = []
    for bi in range(bb):
        zq = zq_ref[bi]
        cbuf[bi, SUBLANES:, :] = zq[:, :2 * D_A]
        acc = zq[:, :2 * D_A] * cw_ref[CONV_W - 1:CONV_W, :]
        for j in range(CONV_W - 1):
            d = CONV_W - 1 - j
            acc = acc + cbuf[bi, SUBLANES - d:SUBLANES - d + lc, :] * cw_ref[j:j + 1, :]
        qk = acc * _sigmoid(acc)
        last_rows = cbuf[bi, lc:lc + SUBLANES, :]
        cbuf[bi, 0:SUBLANES, :] = last_rows
        cv_ref[bi] = last_rows

        logi = zi_ref[bi] + bi_ref[...]
        logf = -_softplus(-(zf_ref[bi] + bf_ref[...]))
        if npad:
            pos = c * lc + lax.broadcasted_iota(jnp.int32, (lc, 1), 0)
            logi = jnp.where(pos >= npad, logi, NEG)
            logf = jnp.where(pos >= npad, logf, 0.0)
        bcum = _cumsum_rows(tril_b, logf)
        gcol = logi - bcum
        g_hi = _bf(gcol)
        g_lo = _bf(gcol - g_hi.astype(F32))
        grow_all = _nt(eye8, g_hi) + _nt(eye8, g_lo)
        prep.append(dict(bcum=bcum, gcol=gcol, grow=grow_all, m_row=m_ref[bi], n_all=n_ref[bi],
                         q=_bf(qk[:, :D_A]), k=_bf(qk[:, D_A:] * (HEAD_DIM ** -0.5)), v=zq[:, 2 * D_A:3 * D_A],
                         o=_sigmoid(zq[:, 3 * D_A:4 * D_A])))
    chains = [(bi, h) for bi in range(bb) for h in range(H_A)]
    n = range(len(chains))
    hs = lambda name: [prep[bi][name][:, h * HEAD_DIM:(h + 1) * HEAD_DIM] for bi, h in chains]
    col = lambda name: [prep[bi][name][:, h:h + 1] for bi, h in chains]
    c0 = [c_ref[bi, h] for bi, h in chains]
    n0 = [prep[bi]['n_all'][h:h + 1, :] for bi, h in chains]
    q, k, v = hs('q'), hs('k'), hs('v')
    bcol, gcl, m0 = col('bcum'), col('gcol'), col('m_row')
    dm = [jnp.where(causal, bcol[i] + prep[bi]['grow'][h:h + 1, :], NEG) for i, (bi, h) in enumerate(chains)]
    g = [bcol[i] + m0[i] for i in n]
    m = [jnp.maximum(g[i], jnp.max(dm[i], axis=-1, keepdims=True)) for i in n]
    w_inter = [jnp.exp(g[i] - m[i]) for i in n]
    s = [_nt(q[i], k[i]) * jnp.exp(dm[i] - m[i]) for i in n]
    num = [w_inter[i] * _nt(q[i], _bf(c0[i])) + _mm(_bf(s[i]), _bf(v[i])) for i in n]
    den = [w_inter[i] * jnp.sum(q[i].astype(F32) * n0[i], axis=-1, keepdims=True)
           + jnp.sum(s[i], axis=-1, keepdims=True) for i in n]
    hh = [num[i] / jnp.maximum(jnp.abs(den[i]), jnp.exp(-m[i])) for i in n]
    m_end = [m[i][lc - 1:lc, :] for i in n]
    w_end = [jnp.exp(bcol[i][lc - 1:lc, :] + gcl[i] - m_end[i]) for i in n]
    dec = [jnp.exp(g[i][lc - 1:lc, :] - m_end[i]) for i in n]
    c1 = [dec[i] * c0[i] + _tn(_bf(v[i] * w_end[i]), k[i]) for i in n]
    n1 = [dec[i] * n0[i] + jnp.sum(k[i].astype(F32) * w_end[i], axis=0, keepdims=True) for i in n]
    for i, (bi, h) in enumerate(chains):
        c_ref[bi, h] = c1[i]
    for bi in range(bb):
        m_new = prep[bi]['m_row']
        n_new = prep[bi]['n_all']
        for h in range(H_A):
            m_new = jnp.where(lane == h, m_end[bi * H_A + h], m_new)
            n_new = jnp.where(sub == h, n1[bi * H_A + h], n_new)
        m_ref[bi] = m_new
        n_ref[bi] = n_new
        hcat = jnp.concatenate(hh[bi * H_A:(bi + 1) * H_A], axis=1)
        mu = _mm(_bf(hcat), ind)
        dev = hcat - mu
        var = _mm(_bf(dev * dev), ind)
        y_ref[bi] = (prep[bi]['o'] * (dev * lax.rsqrt(var + LN_EPS) * ng_ref[...])).astype(y_ref.dtype)


def mlstm_mixer(z3, c0, n0, m0, cv0, conv_w, gate_b, norm_g, lc, npad):
    b, lp, _ = z3.shape
    nc = lp // lc
    n0p = jnp.zeros((b, SUBLANES, HEAD_DIM), F32).at[:, :H_A].set(n0)
    m0p = jnp.zeros((b, 1, LANES), F32).at[:, 0, :H_A].set(m0)
    cv0p = jnp.zeros((b, SUBLANES, 2 * D_A), F32).at[:, SUBLANES - (CONV_W - 1):].set(cv0)
    cwp = jnp.zeros((SUBLANES, 2 * D_A), F32).at[:CONV_W].set(conv_w)
    bi = jnp.zeros((1, LANES), F32).at[0, :H_A].set(gate_b[0])
    bf = jnp.zeros((1, LANES), F32).at[0, :H_A].set(gate_b[1])
    bb = BATCH_PER_STEP
    assert b % bb == 0
    st = lambda *shape: pl.BlockSpec((bb,) + shape, lambda i, j: (i,) + (0,) * len(shape))
    par = lambda *shape: pl.BlockSpec(shape, lambda i, j: (0,) * len(shape))
    y, c1, n1, m1, cv1 = pl.pallas_call(
        functools.partial(_mlstm_kernel, lc=lc, npad=npad, bb=bb),
        out_shape=(jax.ShapeDtypeStruct((b, lp, D_A), BF16),
                   jax.ShapeDtypeStruct((b, H_A, HEAD_DIM, HEAD_DIM), F32),
                   jax.ShapeDtypeStruct((b, SUBLANES, HEAD_DIM), F32),
                   jax.ShapeDtypeStruct((b, 1, LANES), F32),
                   jax.ShapeDtypeStruct((b, SUBLANES, 2 * D_A), F32)),
        grid=(b // bb, nc),
        in_specs=[pl.BlockSpec((bb, lc, ZA_W), lambda i, j: (i, j, Z_A0 // ZA_W)),
                  pl.BlockSpec((bb, lc, LANES), lambda i, j: (i, j, Z_GI0 // LANES)),
                  pl.BlockSpec((bb, lc, LANES), lambda i, j: (i, j, Z_GF0 // LANES)),
                  st(H_A, HEAD_DIM, HEAD_DIM), st(SUBLANES, HEAD_DIM), st(1, LANES), st(SUBLANES, 2 * D_A),
                  par(SUBLANES, 2 * D_A), par(1, LANES), par(1, LANES), par(1, D_A)],
        out_specs=(pl.BlockSpec((bb, lc, D_A), lambda i, j: (i, j, 0)),
                   st(H_A, HEAD_DIM, HEAD_DIM), st(SUBLANES, HEAD_DIM), st(1, LANES), st(SUBLANES, 2 * D_A)),
        scratch_shapes=[pltpu.VMEM((bb, lc + SUBLANES, 2 * D_A), F32)],
        compiler_params=pltpu.CompilerParams(dimension_semantics=("arbitrary", "arbitrary"),
                                             vmem_limit_bytes=VMEM_LIMIT),
        name="mlstm",
    )(z3, z3, z3, c0, n0p, m0p, cv0p, cwp, bi, bf, norm_g.reshape(1, D_A))
    return y, (c1, n1[:, :H_A], m1[:, 0, :H_A], cv1[:, SUBLANES - (CONV_W - 1):])


def _rwkv_kernel(zb_ref, s0_ref, sh0_ref, mu_ref, vec_ref, wup_ref, aup_ref, gup_ref,
                 y_ref, s_ref, sh_ref, *, lc, bb):
    c = pl.program_id(1)

    @pl.when(c == 0)
    def _():
        s_ref[...] = s0_ref[...]
        sh_ref[...] = sh0_ref[...]

    row = lax.broadcasted_iota(jnp.int32, (lc, 1), 0)
    w0, a0, k_k, k_a, r_k, gn_g = (vec_ref[i:i + 1, :] for i in range(6))
    ones_h = _head_indicator(D_B, 1.0).astype(BF16)
    ind = _head_indicator(D_B, 1.0 / HEAD_DIM).astype(BF16)
    incl = _tril(lc)
    tril_b = incl.astype(BF16)
    r2 = lax.broadcasted_iota(jnp.int32, (lc, 2 * lc), 0)
    c2 = lax.broadcasted_iota(jnp.int32, (lc, 2 * lc), 1)
    c2 = jnp.where(c2 >= lc, c2 - lc, c2)
    strict2 = r2 > c2
    incl2 = r2 >= c2
    zeros_h = jnp.zeros((lc, HEAD_DIM), BF16)
    n_dbl = int(math.log2(lc)) - 1
    prep = []
    for bi in range(bb):
        z = zb_ref[bi]
        zprev = jnp.where(row == 0, sh_ref[bi, SUBLANES - 1:SUBLANES, :], pltpu.roll(z, 1, axis=0))
        zs = z + mu_ref[...] * (zprev - z)
        sh_ref[bi] = z[lc - SUBLANES:lc, :]
        r = zs[:, 0:D_B]
        k = zs[:, D_B:2 * D_B]
        v = zs[:, 2 * D_B:3 * D_B]
        xw = zs[:, 3 * D_B:3 * D_B + LANES]
        xa = zs[:, 3 * D_B + LANES:3 * D_B + 2 * LANES]
        xg = zs[:, 3 * D_B + 2 * LANES:3 * D_B + 3 * LANES]
        w_log = -_softplus(-(w0 + _mm(_bf(jnp.tanh(xw)), wup_ref[...]))) - 0.5
        lw = -jnp.exp(w_log)
        a = _sigmoid(a0 + _mm(_bf(xa), aup_ref[...]))
        g = _mm(_bf(_sigmoid(xg)), gup_ref[...])
        kk = k * k_k
        kk = kk / jnp.maximum(jnp.sqrt(_mm(_bf(kk * kk), ones_h)), 1e-12)
        k2 = k * (1.0 + (a - 1.0) * k_a)
        bv = kk * a
        cum = _cumsum_rows(tril_b, lw)
        c_last = cum[lc - 1:lc, :]
        e_neg = jnp.exp(-cum)
        e_end = jnp.exp(c_last - cum)
        rt = _bf(r * jnp.exp(cum))
        at = _bf(-kk * jnp.exp(cum - lw))
        bt = _bf(bv * e_neg)
        kt = _bf(k2 * e_neg)
        bh = _bf(bv * e_end)
        kh = _bf(k2 * e_end)
        vb = _bf(v)
        prep.append(dict(rt=rt, at=at, bt=bt, kt=kt, bh=bh, kh=kh, vb=vb, p_last=jnp.exp(c_last),
                         tail=(gn_g, _mm(_bf(r * k2 * r_k), ones_h) * v, g)))
    chains = [(bi, h) for bi in range(bb) for h in range(H_B)]
    hs = lambda name: [prep[bi][name][:, h * HEAD_DIM:(h + 1) * HEAD_DIM] for bi, h in chains]
    s0 = [s_ref[bi, h] for bi, h in chains]
    at, rt, bt, kt, bh, kh, vb, p_last = (hs(n) for n in ('at', 'rt', 'bt', 'kt', 'bh', 'kh', 'vb', 'p_last'))
    n = range(len(chains))
    lhs = [jnp.concatenate([at[i], rt[i]], axis=0) for i in n]
    pair = [_nt(lhs[i], jnp.concatenate([bt[i], kt[i]], axis=0)) for i in n]
    a_mat = [jnp.where(strict2, pair[i][:lc], 0.0) for i in n]
    b_mat = [jnp.where(incl2, pair[i][lc:], 0.0) for i in n]
    ls = [_nt(lhs[i], _bf(s0[i])) for i in n]
    w = [ls[i][:lc] + _mm(_bf(a_mat[i]), jnp.concatenate([zeros_h, vb[i]], axis=0)) for i in n]
    pw = [a_mat[i][:, :lc] for i in n]
    u = [w[i] + _mm(_bf(pw[i]), _bf(w[i])) for i in n]
    for _ in range(n_dbl):
        pw = [_mm(_bf(pw[i]), _bf(pw[i])) for i in n]
        u = [u[i] + _mm(_bf(pw[i]), _bf(u[i])) for i in n]
    uv = [jnp.concatenate([_bf(u[i]), vb[i]], axis=0) for i in n]
    yh = [ls[i][lc:] + _mm(_bf(b_mat[i]), uv[i]) for i in n]
    s1 = [s0[i] * p_last[i] + _tn(uv[i], jnp.concatenate([bh[i], kh[i]], axis=0)) for i in n]
    for i, (bi, h) in enumerate(chains):
        s_ref[bi, h] = s1[i]
    for bi in range(bb):
        y = jnp.concatenate(yh[bi * H_B:(bi + 1) * H_B], axis=1)
        gn_g_, bonus, g = prep[bi]['tail']
        mu = _mm(_bf(y), ind)
        dev = y - mu
        var = _mm(_bf(dev * dev), ind)
        y_ref[bi] = ((dev * lax.rsqrt(var + RWKV_GN_EPS) * gn_g_ + bonus) * g).astype(y_ref.dtype)


def _pad_b_cols(t):
    o = 3 * D_B
    pad = jnp.zeros(t.shape[:-1] + (LANES - R_W,), t.dtype)
    return jnp.concatenate([t[..., :o + R_W], pad, t[..., o + R_W:o + R_W + R_A], pad, t[..., o + R_W + R_A:]], axis=-1)


def _unpad_b_cols(t):
    o = 3 * D_B
    return jnp.concatenate([t[..., :o + R_W], t[..., o + LANES:o + LANES + R_A], t[..., o + 2 * LANES:]], axis=-1)


def rwkv_mixer(z3, s0, sh0, mu, vecs, w_up, a_up, g_up, lc):
    b, lp, _ = z3.shape
    nc = lp // lc
    sh0p = jnp.zeros((b, SUBLANES, ZB_W), F32).at[:, SUBLANES - 1:].set(_pad_b_cols(sh0))
    mup = _pad_b_cols(mu.reshape(1, N_B_COLS))
    vecp = jnp.zeros((SUBLANES, D_B), F32).at[:6].set(vecs)
    wupp = jnp.zeros((LANES, D_B), BF16).at[:R_W].set(w_up.astype(BF16))
    aupp = jnp.zeros((LANES, D_B), BF16).at[:R_A].set(a_up.astype(BF16))
    bb = BATCH_PER_STEP
    assert b % bb == 0
    st = lambda *shape: pl.BlockSpec((bb,) + shape, lambda i, j: (i,) + (0,) * len(shape))
    par = lambda *shape: pl.BlockSpec(shape, lambda i, j: (0,) * len(shape))
    y, s1, sh1 = pl.pallas_call(
        functools.partial(_rwkv_kernel, lc=lc, bb=bb),
        out_shape=(jax.ShapeDtypeStruct((b, lp, D_B), BF16),
                   jax.ShapeDtypeStruct((b, H_B, HEAD_DIM, HEAD_DIM), F32),
                   jax.ShapeDtypeStruct((b, SUBLANES, ZB_W), F32)),
        grid=(b // bb, nc),
        in_specs=[pl.BlockSpec((bb, lc, ZB_W), lambda i, j: (i, j, Z_B0 // ZB_W)),
                  st(H_B, HEAD_DIM, HEAD_DIM), st(SUBLANES, ZB_W),
                  par(1, ZB_W), par(SUBLANES, D_B), par(LANES, D_B), par(LANES, D_B), par(R_G, D_B)],
        out_specs=(pl.BlockSpec((bb, lc, D_B), lambda i, j: (i, j, 0)),
                   st(H_B, HEAD_DIM, HEAD_DIM), st(SUBLANES, ZB_W)),
        compiler_params=pltpu.CompilerParams(dimension_semantics=("arbitrary", "arbitrary"),
                                             vmem_limit_bytes=VMEM_LIMIT),
        name="rwkv",
    )(z3, s0, sh0p, mup, vecp, wupp, aupp, g_up.astype(BF16))
    return y, (s1, _unpad_b_cols(sh1[:, SUBLANES - 1:]))


def _s5_kernel(u_ref, hr0_ref, hi0_ref, are_ref, aim_ref, ldt_ref, bre_ref, bim_ref, cre_ref, cim_ref,
               d_ref, wglu_ref, y_ref, hr_ref, hi_ref, bbr, bbi, *, lc, bb):
    c = pl.program_id(1)
    a_re = are_ref[...]
    a_im = aim_ref[...]
    dt = jnp.exp(ldt_ref[...])
    mag = jnp.exp(a_re * dt)
    ab_re = mag * jnp.cos(a_im * dt)
    ab_im = mag * jnp.sin(a_im * dt)

    @pl.when(c == 0)
    def _():
        hr_ref[...] = hr0_ref[...]
        hi_ref[...] = hi0_ref[...]
        inv = 1.0 / (a_re * a_re + a_im * a_im)
        q_re = ((ab_re - 1.0) * a_re + ab_im * a_im) * inv
        q_im = (ab_im * a_re - (ab_re - 1.0) * a_im) * inv
        bbr[...] = _bf(q_re * bre_ref[...] - q_im * bim_ref[...])
        bbi[...] = _bf(q_re * bim_ref[...] + q_im * bre_ref[...])

    u = u_ref[...].reshape(bb * lc, D_C)
    row = lax.broadcasted_iota(jnp.int32, (bb * lc, 1), 0)
    pos = jnp.bitwise_and(row, lc - 1)
    ub = _bf(u)
    x_re = _mm(ub, bbr[...])
    x_im = _mm(ub, bbi[...])
    for bi in range(bb):
        h_re = hr_ref[bi]
        h_im = hi_ref[bi]
        first = row == bi * lc
        x_re = x_re + jnp.where(first, ab_re * h_re - ab_im * h_im, 0.0)
        x_im = x_im + jnp.where(first, ab_re * h_im + ab_im * h_re, 0.0)
    p_re, p_im = ab_re, ab_im
    shift = 1
    while shift < lc:
        keep = pos >= shift
        s_re = jnp.where(keep, pltpu.roll(x_re, shift, axis=0), 0.0)
        s_im = jnp.where(keep, pltpu.roll(x_im, shift, axis=0), 0.0)
        x_re, x_im = x_re + p_re * s_re - p_im * s_im, x_im + p_re * s_im + p_im * s_re
        p_re, p_im = p_re * p_re - p_im * p_im, 2.0 * p_re * p_im
        shift *= 2
    for bi in range(bb):
        hr_ref[bi] = x_re[(bi + 1) * lc - 1:(bi + 1) * lc, :]
        hi_ref[bi] = x_im[(bi + 1) * lc - 1:(bi + 1) * lc, :]
    y = _mm(_bf(x_re), cre_ref[...]) - _mm(_bf(x_im), cim_ref[...]) + d_ref[...] * u
    y = 0.5 * y * (1.0 + jnp.tanh(math.sqrt(2.0 / math.pi) * (y + 0.044715 * (y * y * y))))
    y = y * _sigmoid(_mm(_bf(y), wglu_ref[...]))
    y_ref[...] = y.reshape(bb, lc, D_C).astype(y_ref.dtype)


def _block_diag(t):
    g, r, c = t.shape
    eye = jnp.eye(g, dtype=t.dtype)
    return (t[:, :, None, :] * eye[:, None, :, None]).reshape(g * r, g * c)


def s5_mixer(z3, hr0, hi0, a_re, a_im, log_dt, b_re, b_im, c_re, c_im, d_skip, w_glu, lc):
    b, lp, _ = z3.shape
    nc = lp // lc
    gp = G_C * P_C
    flat = lambda t: t.reshape(1, gp).astype(F32)
    ldt = jnp.broadcast_to(log_dt.astype(F32)[:, None], (G_C, P_C)).reshape(1, gp)
    bre = _block_diag(jnp.swapaxes(b_re, 1, 2))
    bim = _block_diag(jnp.swapaxes(b_im, 1, 2))
    cre = _block_diag(jnp.swapaxes(c_re, 1, 2))
    cim = _block_diag(jnp.swapaxes(c_im, 1, 2))
    bb = BATCH_PER_STEP
    assert b % bb == 0 and lc & (lc - 1) == 0
    st = lambda *shape: pl.BlockSpec((bb,) + shape, lambda i, j: (i,) + (0,) * len(shape))
    par = lambda *shape: pl.BlockSpec(shape, lambda i, j: (0,) * len(shape))
    y, hr1, hi1 = pl.pallas_call(
        functools.partial(_s5_kernel, lc=lc, bb=bb),
        out_shape=(jax.ShapeDtypeStruct((b, lp, D_C), BF16),
                   jax.ShapeDtypeStruct((b, 1, gp), F32),
                   jax.ShapeDtypeStruct((b, 1, gp), F32)),
        grid=(b // bb, nc),
        in_specs=[pl.BlockSpec((bb, lc, D_C), lambda i, j: (i, j, Z_C0 // D_C)),
                  st(1, gp), st(1, gp), par(1, gp), par(1, gp), par(1, gp),
                  par(D_C, gp), par(D_C, gp), par(gp, D_C), par(gp, D_C), par(1, D_C), par(D_C, D_C)],
        out_specs=(pl.BlockSpec((bb, lc, D_C), lambda i, j: (i, j, 0)), st(1, gp), st(1, gp)),
        scratch_shapes=[pltpu.VMEM((D_C, gp), BF16), pltpu.VMEM((D_C, gp), BF16)],
        compiler_params=pltpu.CompilerParams(dimension_semantics=("arbitrary", "arbitrary"),
                                             vmem_limit_bytes=VMEM_LIMIT),
        name="s5",
    )(z3, hr0.reshape(b, 1, gp), hi0.reshape(b, 1, gp), flat(a_re), flat(a_im), ldt,
      bre, bim, cre.astype(BF16), cim.astype(BF16), d_skip.reshape(1, D_C), w_glu.astype(BF16))
    return y, (hr1.reshape(b, G_C, P_C), hi1.reshape(b, G_C, P_C))


def _layer_norm(x, g, b):
    mu = jnp.mean(x, axis=-1, keepdims=True)
    dev = x - mu
    var = jnp.mean(dev * dev, axis=-1, keepdims=True)
    return dev * lax.rsqrt(var + LN_EPS) * g + b


def _argmax_lane(x, lane):
    m = jnp.max(x, axis=-1, keepdims=True)
    idx = jnp.min(jnp.where(x == m, lane, LANES), axis=-1, keepdims=True)
    return m, idx


def _out_proj_kernel(x_ref, ya_ref, yb_ref, yc_ref, wa_ref, wb_ref, wc_ref, g_ref, b_ref,
                     wco_ref, bco_ref, wfi_ref, bfi_ref, x1_ref, rt_ref):
    mix = _mm(ya_ref[...], wa_ref[...]) + _mm(yb_ref[...], wb_ref[...]) + _mm(yc_ref[...], wc_ref[...])
    x1 = _layer_norm(DEEP_ALPHA * x_ref[...] + mix, g_ref[...], b_ref[...])
    x1_ref[...] = x1
    tm = x1.shape[0]
    lane = lax.broadcasted_iota(jnp.int32, (tm, LANES), 1)
    lco = jnp.where(lane < N_GROUPS, _mm(x1, wco_ref[...], HI) + bco_ref[...], NEG)
    mco, grp = _argmax_lane(lco, lane)
    p_grp = 1.0 / jnp.sum(jnp.exp(lco - mco), axis=-1, keepdims=True)
    in_grp = (lane >= grp * EXP_PER_GROUP) & (lane < (grp + 1) * EXP_PER_GROUP)
    lfi = jnp.where(in_grp, _mm(x1, wfi_ref[...], HI) + bfi_ref[...], NEG)
    m0, e0 = _argmax_lane(lfi, lane)
    m1, e1 = _argmax_lane(jnp.where(lane == e0, NEG, lfi), lane)
    t1 = jnp.exp(m1 - m0)
    g0 = p_grp / (1.0 + t1)
    g1 = p_grp * t1 / (1.0 + t1)
    rt_ref[...] = jnp.where(lane == 0, e0.astype(F32),
                            jnp.where(lane == 1, e1.astype(F32),
                                      jnp.where(lane == 2, g0, jnp.where(lane == 3, g1, 0.0))))


def out_proj_router(x2d, ya, yb, yc, w_out_bf16, ln_g, ln_b, w_coarse, b_coarse, w_fine, b_fine):
    t = x2d.shape[0]
    tm = min(TM, t)
    wco = jnp.zeros((D_MODEL, LANES), F32).at[:, :N_GROUPS].set(w_coarse)
    bco = jnp.zeros((1, LANES), F32).at[0, :N_GROUPS].set(b_coarse)
    wfi = jnp.zeros((D_MODEL, LANES), F32).at[:, :N_EXP].set(w_fine)
    bfi = jnp.zeros((1, LANES), F32).at[0, :N_EXP].set(b_fine)
    tok = lambda w: pl.BlockSpec((tm, w), lambda i: (i, 0))
    par = lambda *shape: pl.BlockSpec(shape, lambda i: (0,) * len(shape))
    return pl.pallas_call(
        _out_proj_kernel,
        out_shape=(jax.ShapeDtypeStruct((t, D_MODEL), F32), jax.ShapeDtypeStruct((t, LANES), F32)),
        grid=(t // tm,),
        in_specs=[tok(D_MODEL), tok(D_A), tok(D_B), tok(D_C),
                  par(D_A, D_MODEL), par(D_B, D_MODEL), par(D_C, D_MODEL), par(1, D_MODEL), par(1, D_MODEL),
                  par(D_MODEL, LANES), par(1, LANES), par(D_MODEL, LANES), par(1, LANES)],
        out_specs=(tok(D_MODEL), tok(LANES)),
        compiler_params=pltpu.CompilerParams(dimension_semantics=("arbitrary",), vmem_limit_bytes=VMEM_LIMIT),
        name="out_proj_router",
    )(x2d, ya, yb, yc, w_out_bf16[:D_A], w_out_bf16[D_A:D_A + D_B], w_out_bf16[D_A + D_B:],
      ln_g.reshape(1, D_MODEL), ln_b.reshape(1, D_MODEL), wco, bco, wfi, bfi)


def _row_copy(src_hbm, dst, sem, tok, r):
    return pltpu.make_async_copy(src_hbm.at[pl.ds(tok, 1)], dst.at[pl.ds(r, 1)], sem)


def _ffn_kernel(be_ref, rt_ref, x_hbm, wg_ref, wu_ref, wd_ref, ys_ref, xbuf, sem):
    i = pl.program_id(0)
    n = pl.num_programs(0)
    slot = i % 2

    def gather(block, slot_):
        def body(r, carry):
            _row_copy(x_hbm, xbuf.at[slot_], sem.at[slot_], rt_ref[block * MOE_BLOCK + r], r).start()
            return carry
        lax.fori_loop(0, MOE_BLOCK, body, 0)

    @pl.when(i == 0)
    def _():
        gather(0, 0)

    @pl.when(i + 1 < n)
    def _():
        gather(i + 1, 1 - slot)

    pltpu.make_async_copy(x_hbm.at[pl.ds(0, MOE_BLOCK)], xbuf.at[slot], sem.at[slot]).wait()

    xb = xbuf[slot].astype(BF16)
    hg = _mm(xb, wg_ref[...])
    hu = _mm(xb, wu_ref[...])
    h = (hg * _sigmoid(hg) * hu).astype(BF16)
    ys_ref[...] = _mm(h, wd_ref[...])


def expert_ffn(x1, block_e, row_tok, wg, wu, wd):
    n_rows = row_tok.shape[0]
    n_blocks = n_rows // MOE_BLOCK
    grid_spec = pltpu.PrefetchScalarGridSpec(
        num_scalar_prefetch=2,
        grid=(n_blocks,),
        in_specs=[pl.BlockSpec(memory_space=pl.ANY),
                  pl.BlockSpec((None, D_MODEL, D_EXPERT), lambda i, be, rt: (be[i], 0, 0)),
                  pl.BlockSpec((None, D_MODEL, D_EXPERT), lambda i, be, rt: (be[i], 0, 0)),
                  pl.BlockSpec((None, D_EXPERT, D_MODEL), lambda i, be, rt: (be[i], 0, 0))],
        out_specs=pl.BlockSpec((MOE_BLOCK, D_MODEL), lambda i, be, rt: (i, 0)),
        scratch_shapes=[pltpu.VMEM((2, MOE_BLOCK, D_MODEL), F32), pltpu.SemaphoreType.DMA((2,))],
    )
    return pl.pallas_call(
        _ffn_kernel,
        out_shape=jax.ShapeDtypeStruct((n_rows, D_MODEL), F32),
        grid_spec=grid_spec,
        compiler_params=pltpu.CompilerParams(dimension_semantics=("arbitrary",), vmem_limit_bytes=VMEM_LIMIT),
        name="expert_ffn",
    )(block_e, row_tok, x1, wg, wu, wd)


def _combine_kernel(d_ref, x1_ref, rt_ref, g_ref, b_ref, ys_hbm, o_ref, buf, sem, *, tm):
    i = pl.program_id(0)
    n = pl.num_programs(0)
    slot = i % 2

    def gather(tile, slot_):
        def body(r, carry):
            base = (tile * tm + r) * TOP_K
            _row_copy(ys_hbm, buf.at[slot_, 0], sem.at[slot_], d_ref[base], r).start()
            _row_copy(ys_hbm, buf.at[slot_, 1], sem.at[slot_], d_ref[base + 1], r).start()
            return carry
        lax.fori_loop(0, tm, body, 0)

    @pl.when(i == 0)
    def _():
        gather(0, 0)

    @pl.when(i + 1 < n)
    def _():
        gather(i + 1, 1 - slot)

    for j in range(TOP_K):
        pltpu.make_async_copy(ys_hbm.at[pl.ds(0, tm)], buf.at[slot, j], sem.at[slot]).wait()

    rt = rt_ref[...]
    ffn = rt[:, 2:3] * buf[slot, 0] + rt[:, 3:4] * buf[slot, 1]
    o_ref[...] = _layer_norm(DEEP_ALPHA * x1_ref[...] + ffn, g_ref[...], b_ref[...])


def combine_ln(x1, route, dest, ys, ln_g, ln_b):
    t = x1.shape[0]
    tm = min(TM, t)
    grid_spec = pltpu.PrefetchScalarGridSpec(
        num_scalar_prefetch=1,
        grid=(t // tm,),
        in_specs=[pl.BlockSpec((tm, D_MODEL), lambda i, d: (i, 0)),
                  pl.BlockSpec((tm, LANES), lambda i, d: (i, 0)),
                  pl.BlockSpec((1, D_MODEL), lambda i, d: (0, 0)),
                  pl.BlockSpec((1, D_MODEL), lambda i, d: (0, 0)),
                  pl.BlockSpec(memory_space=pl.ANY)],
        out_specs=pl.BlockSpec((tm, D_MODEL), lambda i, d: (i, 0)),
        scratch_shapes=[pltpu.VMEM((2, TOP_K, tm, D_MODEL), F32), pltpu.SemaphoreType.DMA((2,))],
    )
    return pl.pallas_call(
        functools.partial(_combine_kernel, tm=tm),
        out_shape=jax.ShapeDtypeStruct((t, D_MODEL), F32),
        grid_spec=grid_spec,
        compiler_params=pltpu.CompilerParams(dimension_semantics=("arbitrary",), vmem_limit_bytes=VMEM_LIMIT),
        name="combine_ln",
    )(dest, x1, route, ln_g.reshape(1, D_MODEL), ln_b.reshape(1, D_MODEL), ys)


def route_tables(route):
    t = route.shape[0]
    a = t * TOP_K
    n_blocks = -(-a // MOE_BLOCK) + N_EXP
    flat_e = route[:, :TOP_K].astype(jnp.int32).reshape(-1)
    onehot = (flat_e[:, None] == jnp.arange(N_EXP, dtype=jnp.int32)[None, :]).astype(jnp.int32)
    csum = jnp.cumsum(onehot, axis=0)
    counts = csum[-1]
    rank = jnp.take_along_axis(csum, flat_e[:, None], axis=1)[:, 0] - 1
    padded = (counts + MOE_BLOCK - 1) // MOE_BLOCK * MOE_BLOCK
    pad_end = jnp.cumsum(padded)
    pad_start = pad_end - padded
    dest = (pad_start[flat_e] + rank).astype(jnp.int32)
    row_tok = jnp.zeros((n_blocks * MOE_BLOCK,), jnp.int32).at[dest].set(jnp.arange(a, dtype=jnp.int32) // TOP_K)
    block_e = jnp.minimum(jnp.searchsorted(pad_end, jnp.arange(n_blocks, dtype=jnp.int32) * MOE_BLOCK, side='right'),
                          N_EXP - 1).astype(jnp.int32)
    return block_e, row_tok, dest


def _prep_w_in(w_in):
    wa = w_in[:, :ZA_W]
    gi = w_in[:, ZA_W:ZA_W + H_A]
    gf = w_in[:, ZA_W + H_A:N_A_COLS]
    wb = _pad_b_cols(w_in[:, N_A_COLS:N_A_COLS + N_B_COLS])
    wc = w_in[:, N_A_COLS + N_B_COLS:]
    gpad = jnp.zeros((D_MODEL, LANES - H_A), w_in.dtype)
    return jnp.concatenate([wa, wb, gi, gpad, gf, gpad, wc], axis=1).astype(BF16)


def _trunk_layer(x3, st, npad, lc, p):
    b, lp, _ = x3.shape
    c0, n0, m0, cv0, s0, sh0, hr0, hi0 = st
    x2d = x3.reshape(b * lp, D_MODEL)
    z3 = in_proj(x2d, p['w_in'], lp, npad).reshape(b, lp, Z_N)
    ya, st_a = mlstm_mixer(z3, c0, n0, m0, cv0, p['mlstm_conv_w'], p['mlstm_gate_b'], p['mlstm_norm_g'],
                           min(CHUNK_MLSTM, lp), npad)
    yb, st_b = rwkv_mixer(z3, s0, sh0, p['rwkv_mu'], p['rwkv_vecs'], p['rwkv_w_up'], p['rwkv_a_up'], p['rwkv_g_up'], lc)
    yc, st_c = s5_mixer(z3, hr0, hi0, p['s5_A_re'], p['s5_A_im'], p['s5_log_dt'], p['s5_B_re'], p['s5_B_im'],
                        p['s5_C_re'], p['s5_C_im'], p['s5_D'], p['s5_w_glu'], lc)
    t = b * lp
    x1, route = out_proj_router(x2d, ya.reshape(t, D_A), yb.reshape(t, D_B), yc.reshape(t, D_C), p['w_out'],
                                p['ln_g'][0], p['ln_b'][0], p['moe_w_coarse'], p['moe_b_coarse'],
                                p['moe_w_fine'], p['moe_b_fine'])
    block_e, row_tok, dest = route_tables(route)
    ys = expert_ffn(x1, block_e, row_tok, p['moe_w_gate'], p['moe_w_up'], p['moe_w_down'])
    x2 = combine_ln(x1, route, dest, ys, p['ln_g'][1], p['ln_b'][1])
    return x2.reshape(b, lp, D_MODEL), st_a + st_b + st_c


def _run_trunk(x3, states, npad, lc, params):
    new = [[] for _ in states]
    for l in range(DEPTH):
        p = {name: arr[l] for name, arr in params.items()}
        x3, st_new = _trunk_layer(x3, [s[l].astype(F32) for s in states], npad, lc, p)
        for lst, s in zip(new, st_new):
            lst.append(s)
    return x3, [jnp.stack(lst) for lst in new]


def kernel(x_prompt, x_sample, state_mlstm_C, state_mlstm_n, state_mlstm_m, state_mlstm_conv, state_rwkv_S, state_rwkv_shift, state_s5_re, state_s5_im, meta_tokens, w_in, w_out, mlstm_conv_w, mlstm_gate_b, mlstm_norm_g, rwkv_mu, rwkv_vecs, rwkv_w_up, rwkv_a_up, rwkv_g_up, s5_A_re, s5_A_im, s5_log_dt, s5_B_re, s5_B_im, s5_C_re, s5_C_im, s5_D, s5_w_glu, ln_g, ln_b, moe_w_coarse, moe_b_coarse, moe_w_fine, moe_b_fine, moe_w_gate, moe_w_up, moe_w_down):
    params = dict(w_in=jax.vmap(_prep_w_in)(w_in), w_out=w_out.astype(BF16), mlstm_conv_w=mlstm_conv_w,
                  mlstm_gate_b=mlstm_gate_b, mlstm_norm_g=mlstm_norm_g, rwkv_mu=rwkv_mu, rwkv_vecs=rwkv_vecs,
                  rwkv_w_up=rwkv_w_up, rwkv_a_up=rwkv_a_up, rwkv_g_up=rwkv_g_up, s5_A_re=s5_A_re, s5_A_im=s5_A_im,
                  s5_log_dt=s5_log_dt, s5_B_re=s5_B_re, s5_B_im=s5_B_im, s5_C_re=s5_C_re, s5_C_im=s5_C_im,
                  s5_D=s5_D, s5_w_glu=s5_w_glu, ln_g=ln_g, ln_b=ln_b, moe_w_coarse=moe_w_coarse,
                  moe_b_coarse=moe_b_coarse, moe_w_fine=moe_w_fine, moe_b_fine=moe_b_fine,
                  moe_w_gate=moe_w_gate.astype(BF16), moe_w_up=moe_w_up.astype(BF16),
                  moe_w_down=moe_w_down.astype(BF16))
    state_in = [state_mlstm_C, state_mlstm_n, state_mlstm_m, state_mlstm_conv,
                state_rwkv_S, state_rwkv_shift, state_s5_re, state_s5_im]

    b, seq, _ = x_prompt.shape
    real = N_META + seq
    npad = (-real) % SEQ_PAD_MULT
    meta = jnp.broadcast_to(meta_tokens.astype(x_prompt.dtype)[None], (b, N_META, D_MODEL))
    xp = jnp.concatenate([jnp.zeros((b, npad, D_MODEL), x_prompt.dtype), meta, x_prompt], axis=1)
    fresh = [jnp.zeros((DEPTH, b) + s.shape[2:], F32) for s in state_in]
    yp, pst = _run_trunk(xp, fresh, npad, CHUNK, params)
    y_prompt = yp[:, npad + N_META:]
    pst = [s.astype(r.dtype) for s, r in zip(pst, state_in)]

    y_sample, sst = _run_trunk(x_sample, state_in, 0, CHUNK, params)
    sst = [s.astype(r.dtype) for s, r in zip(sst, state_in)]
    return (y_prompt, y_sample, *pst, *sst)
```

```python
import functools
import math

import jax
import jax.numpy as jnp
from jax import lax
from jax.experimental import pallas as pl
from jax.experimental.pallas import tpu as pltpu

F32 = jnp.float32
BF16 = jnp.bfloat16
HI = lax.Precision.HIGHEST

D_MODEL = 1024
DEPTH = 2
N_META = 16
HEAD_DIM = 64
D_A = 384
H_A = D_A // HEAD_DIM
CONV_W = 4
D_B = 384
H_B = D_B // HEAD_DIM
R_W = 64
R_A = 64
R_G = 128
RWKV_GN_EPS = 64e-5
D_C = 256
C_GROUP = 16
G_C = D_C // C_GROUP
P_C = 64
D_MIX = D_A + D_B + D_C
N_A_COLS = 4 * D_A + 2 * H_A
N_B_COLS = 3 * D_B + R_W + R_A + R_G
N_GROUPS = 4
EXP_PER_GROUP = 8
N_EXP = N_GROUPS * EXP_PER_GROUP
TOP_K = 2
D_EXPERT = 512
MOE_BLOCK = 256
DEEP_ALPHA = (2 * DEPTH) ** 0.25
LN_EPS = 1e-5

LANES = 128
SUBLANES = 8
VMEM_LIMIT = 48 * 1024 * 1024

ZA_W = 4 * D_A
ZB_W = 3 * D_B + 3 * LANES
Z_A0 = 0
Z_B0 = ZA_W
Z_GI0 = Z_B0 + ZB_W
Z_GF0 = Z_GI0 + LANES
Z_C0 = Z_GF0 + LANES
Z_N = Z_C0 + D_C
SEQ_PAD_MULT = 256
CHUNK = 64
CHUNK_MLSTM = 128
GATHER_SLOTS = 3
BATCH_PER_STEP = 2
TM = 256
NEG = -1e30


def _nt(a, b, precision=None):
    return lax.dot_general(a, b, (((1,), (1,)), ((), ())), precision=precision, preferred_element_type=F32)


def _tn(a, b, precision=None):
    return lax.dot_general(a, b, (((0,), (0,)), ((), ())), precision=precision, preferred_element_type=F32)


def _mm(a, b, precision=None):
    return jnp.dot(a, b, precision=precision, preferred_element_type=F32)


def _bf(x):
    return x.astype(BF16)


def _cumsum_rows(tril_bf16, x):
    hi = _bf(x)
    lo = _bf(x - hi.astype(F32))
    return _mm(tril_bf16, hi) + _mm(tril_bf16, lo)


def _sigmoid(x):
    return 1.0 / (1.0 + jnp.exp(-x))


def _softplus(x):
    return jnp.maximum(x, 0.0) + jnp.log1p(jnp.exp(-jnp.abs(x)))


def _head_indicator(n, scale):
    shift = jnp.int32(int(math.log2(HEAD_DIM)))
    r = lax.shift_right_logical(lax.broadcasted_iota(jnp.int32, (n, n), 0), shift)
    c = lax.shift_right_logical(lax.broadcasted_iota(jnp.int32, (n, n), 1), shift)
    return jnp.where(r == c, scale, 0.0).astype(F32)


def _tril(n, strict=False):
    r = lax.broadcasted_iota(jnp.int32, (n, n), 0)
    c = lax.broadcasted_iota(jnp.int32, (n, n), 1)
    return (r > c) if strict else (r >= c)


def _in_proj_kernel(x_ref, w_ref, z_ref, *, tm, lp, npad):
    z = _mm(x_ref[...].astype(BF16), w_ref[...])
    if npad:
        pos = (pl.program_id(0) * tm) % lp + lax.broadcasted_iota(jnp.int32, (tm, 1), 0)
        z = jnp.where(pos >= npad, z, 0.0)
    z_ref[...] = z


def in_proj(x2d, w_bf16, lp, npad):
    t = x2d.shape[0]
    tm = min(TM, t)
    assert t % tm == 0 and (npad == 0 or lp % tm == 0)
    return pl.pallas_call(
        functools.partial(_in_proj_kernel, tm=tm, lp=lp, npad=npad),
        out_shape=jax.ShapeDtypeStruct((t, Z_N), F32),
        grid=(t // tm,),
        in_specs=[pl.BlockSpec((tm, D_MODEL), lambda i: (i, 0)),
                  pl.BlockSpec((D_MODEL, Z_N), lambda i: (0, 0))],
        out_specs=pl.BlockSpec((tm, Z_N), lambda i: (i, 0)),
        compiler_params=pltpu.CompilerParams(dimension_semantics=("arbitrary",), vmem_limit_bytes=VMEM_LIMIT),
        name="in_proj",
    )(x2d, w_bf16)


def _mlstm_kernel(zq_ref, zi_ref, zf_ref, c0_ref, n0_ref, m0_ref, cv0_ref, cw_ref, bi_ref, bf_ref, ng_ref,
                  y_ref, c_ref, n_ref, m_ref, cv_ref, cbuf, *, lc, npad, bb):
    c = pl.program_id(1)

    @pl.when(c == 0)
    def _():
        c_ref[...] = c0_ref[...]
        n_ref[...] = n0_ref[...]
        m_ref[...] = m0_ref[...]
        cbuf[:, 0:SUBLANES, :] = cv0_ref[...]

    causal = _tril(lc)
    tril_b = causal.astype(BF16)
    eye8 = (lax.broadcasted_iota(jnp.int32, (SUBLANES, LANES), 0)
            == lax.broadcasted_iota(jnp.int32, (SUBLANES, LANES), 1)).astype(BF16)
    lane = lax.broadcasted_iota(jnp.int32, (1, LANES), 1)
    sub = lax.broadcasted_iota(jnp.int32, (SUBLANES, 1), 0)
    ind = _head_indicator(D_A, 1.0 / HEAD_DIM).astype(BF16)
    prep = []
    for bi in range(bb):
        zq = zq_ref[bi]
        cbuf[bi, SUBLANES:, :] = zq[:, :2 * D_A]
        acc = zq[:, :2 * D_A] * cw_ref[CONV_W - 1:CONV_W, :]
        for j in range(CONV_W - 1):
            d = CONV_W - 1 - j
            acc = acc + cbuf[bi, SUBLANES - d:SUBLANES - d + lc, :] * cw_ref[j:j + 1, :]
        qk = acc * _sigmoid(acc)
        last_rows = cbuf[bi, lc:lc + SUBLANES, :]
        cbuf[bi, 0:SUBLANES, :] = last_rows
        cv_ref[bi] = last_rows

        logi = zi_ref[bi] + bi_ref[...]
        logf = -_softplus(-(zf_ref[bi] + bf_ref[...]))
        if npad:
            pos = c * lc + lax.broadcasted_iota(jnp.int32, (lc, 1), 0)
            logi = jnp.where(pos >= npad, logi, NEG)
            logf = jnp.where(pos >= npad, logf, 0.0)
        bcum = _cumsum_rows(tril_b, logf)
        gcol = logi - bcum
        g_hi = _bf(gcol)
        g_lo = _bf(gcol - g_hi.astype(F32))
        grow_all = _nt(eye8, g_hi) + _nt(eye8, g_lo)
        prep.append(dict(bcum=bcum, gcol=gcol, grow=grow_all, m_row=m_ref[bi], n_all=n_ref[bi],
                         q=_bf(qk[:, :D_A]), k=_bf(qk[:, D_A:] * (HEAD_DIM ** -0.5)), v=zq[:, 2 * D_A:3 * D_A],
                         o=_sigmoid(zq[:, 3 * D_A:4 * D_A])))
    chains = [(bi, h) for bi in range(bb) for h in range(H_A)]
    n = range(len(chains))
    hs = lambda name: [prep[bi][name][:, h * HEAD_DIM:(h + 1) * HEAD_DIM] for bi, h in chains]
    col = lambda name: [prep[bi][name][:, h:h + 1] for bi, h in chains]
    c0 = [c_ref[bi, h] for bi, h in chains]
    n0 = [prep[bi]['n_all'][h:h + 1, :] for bi, h in chains]
    q, k, v = hs('q'), hs('k'), hs('v')
    bcol, gcl, m0 = col('bcum'), col('gcol'), col('m_row')
    dm = [jnp.where(causal, bcol[i] + prep[bi]['grow'][h:h + 1, :], NEG) for i, (bi, h) in enumerate(chains)]
    g = [bcol[i] + m0[i] for i in n]
    m = [jnp.maximum(g[i], jnp.max(dm[i], axis=-1, keepdims=True)) for i in n]
    w_inter = [jnp.exp(g[i] - m[i]) for i in n]
    s = [_nt(q[i], k[i]) * jnp.exp(dm[i] - m[i]) for i in n]
    num = [w_inter[i] * _nt(q[i], _bf(c0[i])) + _mm(_bf(s[i]), _bf(v[i])) for i in n]
    den = [w_inter[i] * jnp.sum(q[i].astype(F32) * n0[i], axis=-1, keepdims=True)
           + jnp.sum(s[i], axis=-1, keepdims=True) for i in n]
    hh = [num[i] / jnp.maximum(jnp.abs(den[i]), jnp.exp(-m[i])) for i in n]
    m_end = [m[i][lc - 1:lc, :] for i in n]
    w_end = [jnp.exp(bcol[i][lc - 1:lc, :] + gcl[i] - m_end[i]) for i in n]
    dec = [jnp.exp(g[i][lc - 1:lc, :] - m_end[i]) for i in n]
    c1 = [dec[i] * c0[i] + _tn(_bf(v[i] * w_end[i]), k[i]) for i in n]
    n1 = [dec[i] * n0[i] + jnp.sum(k[i].astype(F32) * w_end[i], axis=0, keepdims=True) for i in n]
    for i, (bi, h) in enumerate(chains):
        c_ref[bi, h] = c1[i]
    for bi in range(bb):
        m_new = prep[bi]['m_row']
        n_new = prep[bi]['n_all']
        for h in range(H_A):
            m_new = jnp.where(lane == h, m_end[bi * H_A + h], m_new)
            n_new = jnp.where(sub == h, n1[bi * H_A + h], n_new)
        m_ref[bi] = m_new
        n_ref[bi] = n_new
        hcat = jnp.concatenate(hh[bi * H_A:(bi + 1) * H_A], axis=1)
        mu = _mm(_bf(hcat), ind)
        dev = hcat - mu
        var = _mm(_bf(dev * dev), ind)
        y_ref[bi] = (prep[bi]['o'] * (dev * lax.rsqrt(var + LN_EPS) * ng_ref[...])).astype(y_ref.dtype)


def mlstm_mixer(z3, c0, n0, m0, cv0, conv_w, gate_b, norm_g, lc, npad):
    b, lp, _ = z3.shape
    nc = lp // lc
    n0p = jnp.zeros((b, SUBLANES, HEAD_DIM), F32).at[:, :H_A].set(n0)
    m0p = jnp.zeros((b, 1, LANES), F32).at[:, 0, :H_A].set(m0)
    cv0p = jnp.zeros((b, SUBLANES, 2 * D_A), F32).at[:, SUBLANES - (CONV_W - 1):].set(cv0)
    cwp = jnp.zeros((SUBLANES, 2 * D_A), F32).at[:CONV_W].set(conv_w)
    bi = jnp.zeros((1, LANES), F32).at[0, :H_A].set(gate_b[0])
    bf = jnp.zeros((1, LANES), F32).at[0, :H_A].set(gate_b[1])
    bb = BATCH_PER_STEP
    assert b % bb == 0
    st = lambda *shape: pl.BlockSpec((bb,) + shape, lambda i, j: (i,) + (0,) * len(shape))
    par = lambda *shape: pl.BlockSpec(shape, lambda i, j: (0,) * len(shape))
    y, c1, n1, m1, cv1 = pl.pallas_call(
        functools.partial(_mlstm_kernel, lc=lc, npad=npad, bb=bb),
        out_shape=(jax.ShapeDtypeStruct((b, lp, D_A), BF16),
                   jax.ShapeDtypeStruct((b, H_A, HEAD_DIM, HEAD_DIM), F32),
                   jax.ShapeDtypeStruct((b, SUBLANES, HEAD_DIM), F32),
                   jax.ShapeDtypeStruct((b, 1, LANES), F32),
                   jax.ShapeDtypeStruct((b, SUBLANES, 2 * D_A), F32)),
        grid=(b // bb, nc),
        in_specs=[pl.BlockSpec((bb, lc, ZA_W), lambda i, j: (i, j, Z_A0 // ZA_W)),
                  pl.BlockSpec((bb, lc, LANES), lambda i, j: (i, j, Z_GI0 // LANES)),
                  pl.BlockSpec((bb, lc, LANES), lambda i, j: (i, j, Z_GF0 // LANES)),
                  st(H_A, HEAD_DIM, HEAD_DIM), st(SUBLANES, HEAD_DIM), st(1, LANES), st(SUBLANES, 2 * D_A),
                  par(SUBLANES, 2 * D_A), par(1, LANES), par(1, LANES), par(1, D_A)],
        out_specs=(pl.BlockSpec((bb, lc, D_A), lambda i, j: (i, j, 0)),
                   st(H_A, HEAD_DIM, HEAD_DIM), st(SUBLANES, HEAD_DIM), st(1, LANES), st(SUBLANES, 2 * D_A)),
        scratch_shapes=[pltpu.VMEM((bb, lc + SUBLANES, 2 * D_A), F32)],
        compiler_params=pltpu.CompilerParams(dimension_semantics=("arbitrary", "arbitrary"),
                                             vmem_limit_bytes=VMEM_LIMIT),
        name="mlstm",
    )(z3, z3, z3, c0, n0p, m0p, cv0p, cwp, bi, bf, norm_g.reshape(1, D_A))
    return y, (c1, n1[:, :H_A], m1[:, 0, :H_A], cv1[:, SUBLANES - (CONV_W - 1):])


def _rwkv_kernel(zb_ref, s0_ref, sh0_ref, mu_ref, vec_ref, wup_ref, aup_ref, gup_ref,
                 y_ref, s_ref, sh_ref, *, lc, bb):
    c = pl.program_id(1)

    @pl.when(c == 0)
    def _():
        s_ref[...] = s0_ref[...]
        sh_ref[...] = sh0_ref[...]

    row = lax.broadcasted_iota(jnp.int32, (lc, 1), 0)
    w0, a0, k_k, k_a, r_k, gn_g = (vec_ref[i:i + 1, :] for i in range(6))
    ones_h = _head_indicator(D_B, 1.0).astype(BF16)
    ind = _head_indicator(D_B, 1.0 / HEAD_DIM).astype(BF16)
    incl = _tril(lc)
    tril_b = incl.astype(BF16)
    r2 = lax.broadcasted_iota(jnp.int32, (lc, 2 * lc), 0)
    c2 = lax.broadcasted_iota(jnp.int32, (lc, 2 * lc), 1)
    c2 = jnp.where(c2 >= lc, c2 - lc, c2)
    strict2 = r2 > c2
    incl2 = r2 >= c2
    zeros_h = jnp.zeros((lc, HEAD_DIM), BF16)
    n_dbl = int(math.log2(lc)) - 1
    prep = []
    for bi in range(bb):
        z = zb_ref[bi]
        zprev = jnp.where(row == 0, sh_ref[bi, SUBLANES - 1:SUBLANES, :], pltpu.roll(z, 1, axis=0))
        zs = z + mu_ref[...] * (zprev - z)
        sh_ref[bi] = z[lc - SUBLANES:lc, :]
        r = zs[:, 0:D_B]
        k = zs[:, D_B:2 * D_B]
        v = zs[:, 2 * D_B:3 * D_B]
        xw = zs[:, 3 * D_B:3 * D_B + LANES]
        xa = zs[:, 3 * D_B + LANES:3 * D_B + 2 * LANES]
        xg = zs[:, 3 * D_B + 2 * LANES:3 * D_B + 3 * LANES]
        w_log = -_softplus(-(w0 + _mm(_bf(jnp.tanh(xw)), wup_ref[...]))) - 0.5
        lw = -jnp.exp(w_log)
        a = _sigmoid(a0 + _mm(_bf(xa), aup_ref[...]))
        g = _mm(_bf(_sigmoid(xg)), gup_ref[...])
        kk = k * k_k
        kk = kk / jnp.maximum(jnp.sqrt(_mm(_bf(kk * kk), ones_h)), 1e-12)
        k2 = k * (1.0 + (a - 1.0) * k_a)
        bv = kk * a
        cum = _cumsum_rows(tril_b, lw)
        c_last = cum[lc - 1:lc, :]
        e_neg = jnp.exp(-cum)
        e_end = jnp.exp(c_last - cum)
        rt = _bf(r * jnp.exp(cum))
        at = _bf(-kk * jnp.exp(cum - lw))
        bt = _bf(bv * e_neg)
        kt = _bf(k2 * e_neg)
        bh = _bf(bv * e_end)
        kh = _bf(k2 * e_end)
        vb = _bf(v)
        prep.append(dict(rt=rt, at=at, bt=bt, kt=kt, bh=bh, kh=kh, vb=vb, p_last=jnp.exp(c_last),
                         tail=(gn_g, _mm(_bf(r * k2 * r_k), ones_h) * v, g)))
    chains = [(bi, h) for bi in range(bb) for h in range(H_B)]
    hs = lambda name: [prep[bi][name][:, h * HEAD_DIM:(h + 1) * HEAD_DIM] for bi, h in chains]
    s0 = [s_ref[bi, h] for bi, h in chains]
    at, rt, bt, kt, bh, kh, vb, p_last = (hs(n) for n in ('at', 'rt', 'bt', 'kt', 'bh', 'kh', 'vb', 'p_last'))
    n = range(len(chains))
    lhs = [jnp.concatenate([at[i], rt[i]], axis=0) for i in n]
    pair = [_nt(lhs[i], jnp.concatenate([bt[i], kt[i]], axis=0)) for i in n]
    a_mat = [jnp.where(strict2, pair[i][:lc], 0.0) for i in n]
    b_mat = [jnp.where(incl2, pair[i][lc:], 0.0) for i in n]
    ls = [_nt(lhs[i], _bf(s0[i])) for i in n]
    w = [ls[i][:lc] + _mm(_bf(a_mat[i]), jnp.concatenate([zeros_h, vb[i]], axis=0)) for i in n]
    pw = [a_mat[i][:, :lc] for i in n]
    u = [w[i] + _mm(_bf(pw[i]), _bf(w[i])) for i in n]
    for _ in range(n_dbl):
        pw = [_mm(_bf(pw[i]), _bf(pw[i])) for i in n]
        u = [u[i] + _mm(_bf(pw[i]), _bf(u[i])) for i in n]
    uv = [jnp.concatenate([_bf(u[i]), vb[i]], axis=0) for i in n]
    yh = [ls[i][lc:] + _mm(_bf(b_mat[i]), uv[i]) for i in n]
    s1 = [s0[i] * p_last[i] + _tn(uv[i], jnp.concatenate([bh[i], kh[i]], axis=0)) for i in n]
    for i, (bi, h) in enumerate(chains):
        s_ref[bi, h] = s1[i]
    for bi in range(bb):
        y = jnp.concatenate(yh[bi * H_B:(bi + 1) * H_B], axis=1)
        gn_g_, bonus, g = prep[bi]['tail']
        mu = _mm(_bf(y), ind)
        dev = y - mu
        var = _mm(_bf(dev * dev), ind)
        y_ref[bi] = ((dev * lax.rsqrt(var + RWKV_GN_EPS) * gn_g_ + bonus) * g).astype(y_ref.dtype)


def _pad_b_cols(t):
    o = 3 * D_B
    pad = jnp.zeros(t.shape[:-1] + (LANES - R_W,), t.dtype)
    return jnp.concatenate([t[..., :o + R_W], pad, t[..., o + R_W:o + R_W + R_A], pad, t[..., o + R_W + R_A:]], axis=-1)


def _unpad_b_cols(t):
    o = 3 * D_B
    return jnp.concatenate([t[..., :o + R_W], t[..., o + LANES:o + LANES + R_A], t[..., o + 2 * LANES:]], axis=-1)


def rwkv_mixer(z3, s0, sh0, mu, vecs, w_up, a_up, g_up, lc):
    b, lp, _ = z3.shape
    nc = lp // lc
    sh0p = jnp.zeros((b, SUBLANES, ZB_W), F32).at[:, SUBLANES - 1:].set(_pad_b_cols(sh0))
    mup = _pad_b_cols(mu.reshape(1, N_B_COLS))
    vecp = jnp.zeros((SUBLANES, D_B), F32).at[:6].set(vecs)
    wupp = jnp.zeros((LANES, D_B), BF16).at[:R_W].set(w_up.astype(BF16))
    aupp = jnp.zeros((LANES, D_B), BF16).at[:R_A].set(a_up.astype(BF16))
    bb = BATCH_PER_STEP
    assert b % bb == 0
    st = lambda *shape: pl.BlockSpec((bb,) + shape, lambda i, j: (i,) + (0,) * len(shape))
    par = lambda *shape: pl.BlockSpec(shape, lambda i, j: (0,) * len(shape))
    y, s1, sh1 = pl.pallas_call(
        functools.partial(_rwkv_kernel, lc=lc, bb=bb),
        out_shape=(jax.ShapeDtypeStruct((b, lp, D_B), BF16),
                   jax.ShapeDtypeStruct((b, H_B, HEAD_DIM, HEAD_DIM), F32),
                   jax.ShapeDtypeStruct((b, SUBLANES, ZB_W), F32)),
        grid=(b // bb, nc),
        in_specs=[pl.BlockSpec((bb, lc, ZB_W), lambda i, j: (i, j, Z_B0 // ZB_W)),
                  st(H_B, HEAD_DIM, HEAD_DIM), st(SUBLANES, ZB_W),
                  par(1, ZB_W), par(SUBLANES, D_B), par(LANES, D_B), par(LANES, D_B), par(R_G, D_B)],
        out_specs=(pl.BlockSpec((bb, lc, D_B), lambda i, j: (i, j, 0)),
                   st(H_B, HEAD_DIM, HEAD_DIM), st(SUBLANES, ZB_W)),
        compiler_params=pltpu.CompilerParams(dimension_semantics=("arbitrary", "arbitrary"),
                                             vmem_limit_bytes=VMEM_LIMIT),
        name="rwkv",
    )(z3, s0, sh0p, mup, vecp, wupp, aupp, g_up.astype(BF16))
    return y, (s1, _unpad_b_cols(sh1[:, SUBLANES - 1:]))


def _s5_kernel(u_ref, hr0_ref, hi0_ref, are_ref, aim_ref, ldt_ref, bre_ref, bim_ref, cre_ref, cim_ref,
               d_ref, wglu_ref, y_ref, hr_ref, hi_ref, bbr, bbi, *, lc, bb):
    c = pl.program_id(1)
    a_re = are_ref[...]
    a_im = aim_ref[...]
    dt = jnp.exp(ldt_ref[...])
    mag = jnp.exp(a_re * dt)
    ab_re = mag * jnp.cos(a_im * dt)
    ab_im = mag * jnp.sin(a_im * dt)

    @pl.when(c == 0)
    def _():
        hr_ref[...] = hr0_ref[...]
        hi_ref[...] = hi0_ref[...]
        inv = 1.0 / (a_re * a_re + a_im * a_im)
        q_re = ((ab_re - 1.0) * a_re + ab_im * a_im) * inv
        q_im = (ab_im * a_re - (ab_re - 1.0) * a_im) * inv
        bbr[...] = _bf(q_re * bre_ref[...] - q_im * bim_ref[...])
        bbi[...] = _bf(q_re * bim_ref[...] + q_im * bre_ref[...])

    u = u_ref[...].reshape(bb * lc, D_C)
    row = lax.broadcasted_iota(jnp.int32, (bb * lc, 1), 0)
    pos = jnp.bitwise_and(row, lc - 1)
    ub = _bf(u)
    x_re = _mm(ub, bbr[...])
    x_im = _mm(ub, bbi[...])
    for bi in range(bb):
        h_re = hr_ref[bi]
        h_im = hi_ref[bi]
        first = row == bi * lc
        x_re = x_re + jnp.where(first, ab_re * h_re - ab_im * h_im, 0.0)
        x_im = x_im + jnp.where(first, ab_re * h_im + ab_im * h_re, 0.0)
    p_re, p_im = ab_re, ab_im
    shift = 1
    while shift < lc:
        keep = pos >= shift
        s_re = jnp.where(keep, pltpu.roll(x_re, shift, axis=0), 0.0)
        s_im = jnp.where(keep, pltpu.roll(x_im, shift, axis=0), 0.0)
        x_re, x_im = x_re + p_re * s_re - p_im * s_im, x_im + p_re * s_im + p_im * s_re
        p_re, p_im = p_re * p_re - p_im * p_im, 2.0 * p_re * p_im
        shift *= 2
    for bi in range(bb):
        hr_ref[bi] = x_re[(bi + 1) * lc - 1:(bi + 1) * lc, :]
        hi_ref[bi] = x_im[(bi + 1) * lc - 1:(bi + 1) * lc, :]
    y = _mm(_bf(x_re), cre_ref[...]) - _mm(_bf(x_im), cim_ref[...]) + d_ref[...] * u
    y = 0.5 * y * (1.0 + jnp.tanh(math.sqrt(2.0 / math.pi) * (y + 0.044715 * (y * y * y))))
    y = y * _sigmoid(_mm(_bf(y), wglu_ref[...]))
    y_ref[...] = y.reshape(bb, lc, D_C).astype(y_ref.dtype)


def _block_diag(t):
    g, r, c = t.shape
    eye = jnp.eye(g, dtype=t.dtype)
    return (t[:, :, None, :] * eye[:, None, :, None]).reshape(g * r, g * c)


def s5_mixer(z3, hr0, hi0, a_re, a_im, log_dt, b_re, b_im, c_re, c_im, d_skip, w_glu, lc):
    b, lp, _ = z3.shape
    nc = lp // lc
    gp = G_C * P_C
    flat = lambda t: t.reshape(1, gp).astype(F32)
    ldt = jnp.broadcast_to(log_dt.astype(F32)[:, None], (G_C, P_C)).reshape(1, gp)
    bre = _block_diag(jnp.swapaxes(b_re, 1, 2))
    bim = _block_diag(jnp.swapaxes(b_im, 1, 2))
    cre = _block_diag(jnp.swapaxes(c_re, 1, 2))
    cim = _block_diag(jnp.swapaxes(c_im, 1, 2))
    bb = BATCH_PER_STEP
    assert b % bb == 0 and lc & (lc - 1) == 0
    st = lambda *shape: pl.BlockSpec((bb,) + shape, lambda i, j: (i,) + (0,) * len(shape))
    par = lambda *shape: pl.BlockSpec(shape, lambda i, j: (0,) * len(shape))
    y, hr1, hi1 = pl.pallas_call(
        functools.partial(_s5_kernel, lc=lc, bb=bb),
        out_shape=(jax.ShapeDtypeStruct((b, lp, D_C), BF16),
                   jax.ShapeDtypeStruct((b, 1, gp), F32),
                   jax.ShapeDtypeStruct((b, 1, gp), F32)),
        grid=(b // bb, nc),
        in_specs=[pl.BlockSpec((bb, lc, D_C), lambda i, j: (i, j, Z_C0 // D_C)),
                  st(1, gp), st(1, gp), par(1, gp), par(1, gp), par(1, gp),
                  par(D_C, gp), par(D_C, gp), par(gp, D_C), par(gp, D_C), par(1, D_C), par(D_C, D_C)],
        out_specs=(pl.BlockSpec((bb, lc, D_C), lambda i, j: (i, j, 0)), st(1, gp), st(1, gp)),
        scratch_shapes=[pltpu.VMEM((D_C, gp), BF16), pltpu.VMEM((D_C, gp), BF16)],
        compiler_params=pltpu.CompilerParams(dimension_semantics=("arbitrary", "arbitrary"),
                                             vmem_limit_bytes=VMEM_LIMIT),
        name="s5",
    )(z3, hr0.reshape(b, 1, gp), hi0.reshape(b, 1, gp), flat(a_re), flat(a_im), ldt,
      bre, bim, cre.astype(BF16), cim.astype(BF16), d_skip.reshape(1, D_C), w_glu.astype(BF16))
    return y, (hr1.reshape(b, G_C, P_C), hi1.reshape(b, G_C, P_C))


def _layer_norm(x, g, b):
    mu = jnp.mean(x, axis=-1, keepdims=True)
    dev = x - mu
    var = jnp.mean(dev * dev, axis=-1, keepdims=True)
    return dev * lax.rsqrt(var + LN_EPS) * g + b


def _argmax_lane(x, lane):
    m = jnp.max(x, axis=-1, keepdims=True)
    idx = jnp.min(jnp.where(x == m, lane, LANES), axis=-1, keepdims=True)
    return m, idx


def _out_proj_kernel(x_ref, ya_ref, yb_ref, yc_ref, wa_ref, wb_ref, wc_ref, g_ref, b_ref,
                     wr_hi_ref, wr_lo_ref, br_ref, x1_ref, rt_ref, cnt_ref):
    @pl.when(pl.program_id(0) == 0)
    def _():
        cnt_ref[...] = jnp.zeros_like(cnt_ref)

    mix = _mm(ya_ref[...], wa_ref[...]) + _mm(yb_ref[...], wb_ref[...]) + _mm(yc_ref[...], wc_ref[...])
    x1 = _layer_norm(DEEP_ALPHA * x_ref[...] + mix, g_ref[...], b_ref[...])
    x1_ref[...] = x1
    tm = x1.shape[0]
    lane = lax.broadcasted_iota(jnp.int32, (tm, LANES), 1)
    x_hi = _bf(x1)
    x_lo = _bf(x1 - x_hi.astype(F32))
    logit = (_mm(x_hi, wr_hi_ref[...]) + _mm(x_hi, wr_lo_ref[...]) + _mm(x_lo, wr_hi_ref[...])) + br_ref[...]
    lco = jnp.where(lane < N_GROUPS, logit, NEG)
    mco, grp = _argmax_lane(lco, lane)
    p_grp = 1.0 / jnp.sum(jnp.exp(lco - mco), axis=-1, keepdims=True)
    lo_lane = N_GROUPS + grp * EXP_PER_GROUP
    lfi = jnp.where((lane >= lo_lane) & (lane < lo_lane + EXP_PER_GROUP), logit, NEG)
    m0, l0 = _argmax_lane(lfi, lane)
    m1, l1 = _argmax_lane(jnp.where(lane == l0, NEG, lfi), lane)
    t1 = jnp.exp(m1 - m0)
    g0 = p_grp / (1.0 + t1)
    g1 = p_grp * t1 / (1.0 + t1)
    onehot = jnp.where((lane == l0) | (lane == l1), 1.0, 0.0)
    before = _mm(_tril(tm, strict=True).astype(BF16), _bf(onehot)) + cnt_ref[...]
    rank0 = jnp.sum(jnp.where(lane == l0, before, 0.0), axis=-1, keepdims=True)
    rank1 = jnp.sum(jnp.where(lane == l1, before, 0.0), axis=-1, keepdims=True)
    cnt_ref[...] += jnp.sum(onehot, axis=0, keepdims=True)
    vals = (l0 - N_GROUPS).astype(F32), (l1 - N_GROUPS).astype(F32), g0, g1, rank0, rank1
    out = jnp.zeros((tm, LANES), F32)
    for j, val in enumerate(vals):
        out = jnp.where(lane == j, val, out)
    rt_ref[...] = out


def out_proj_router(x2d, ya, yb, yc, w_out_bf16, ln_g, ln_b, w_coarse, b_coarse, w_fine, b_fine):
    t = x2d.shape[0]
    tm = min(TM, t)
    wr = jnp.concatenate([w_coarse, w_fine, jnp.zeros((D_MODEL, LANES - N_GROUPS - N_EXP), F32)], axis=1)
    wr_hi = wr.astype(BF16)
    wr_lo = (wr - wr_hi.astype(F32)).astype(BF16)
    br = jnp.concatenate([b_coarse, b_fine, jnp.zeros((LANES - N_GROUPS - N_EXP,), F32)]).reshape(1, LANES)
    tok = lambda w: pl.BlockSpec((tm, w), lambda i: (i, 0))
    par = lambda *shape: pl.BlockSpec(shape, lambda i: (0,) * len(shape))
    x1, route, cnt = pl.pallas_call(
        _out_proj_kernel,
        out_shape=(jax.ShapeDtypeStruct((t, D_MODEL), F32), jax.ShapeDtypeStruct((t, LANES), F32),
                   jax.ShapeDtypeStruct((1, LANES), F32)),
        grid=(t // tm,),
        in_specs=[tok(D_MODEL), tok(D_A), tok(D_B), tok(D_C),
                  par(D_A, D_MODEL), par(D_B, D_MODEL), par(D_C, D_MODEL), par(1, D_MODEL), par(1, D_MODEL),
                  par(D_MODEL, LANES), par(D_MODEL, LANES), par(1, LANES)],
        out_specs=(tok(D_MODEL), tok(LANES), par(1, LANES)),
        compiler_params=pltpu.CompilerParams(dimension_semantics=("arbitrary",), vmem_limit_bytes=VMEM_LIMIT),
        name="out_proj_router",
    )(x2d, ya, yb, yc, w_out_bf16[:D_A], w_out_bf16[D_A:D_A + D_B], w_out_bf16[D_A + D_B:],
      ln_g.reshape(1, D_MODEL), ln_b.reshape(1, D_MODEL), wr_hi, wr_lo, br)
    return x1, route, cnt[0, N_GROUPS:N_GROUPS + N_EXP]


def _row_copy(src_hbm, dst, sem, tok, r):
    return pltpu.make_async_copy(src_hbm.at[pl.ds(tok, 1)], dst.at[pl.ds(r, 1)], sem)


def _dispatch_kernel(d_ref, x1_ref, xs_in, xs_out, sem, *, tm):
    del xs_in
    base = pl.program_id(0) * (tm * TOP_K)
    for r in range(tm):
        for j in range(TOP_K):
            pltpu.make_async_copy(x1_ref.at[pl.ds(r, 1)], xs_out.at[pl.ds(d_ref[base + r * TOP_K + j], 1)], sem).start()
    for j in range(TOP_K):
        pltpu.make_async_copy(x1_ref, xs_out.at[pl.ds(0, tm)], sem).wait()


def dispatch_rows(x1, dest, n_rows):
    t = x1.shape[0]
    tm = min(TM, t)
    grid_spec = pltpu.PrefetchScalarGridSpec(
        num_scalar_prefetch=1,
        grid=(t // tm,),
        in_specs=[pl.BlockSpec((tm, D_MODEL), lambda i, d: (i, 0)), pl.BlockSpec(memory_space=pl.ANY)],
        out_specs=pl.BlockSpec(memory_space=pl.ANY),
        scratch_shapes=[pltpu.SemaphoreType.DMA],
    )
    return pl.pallas_call(
        functools.partial(_dispatch_kernel, tm=tm),
        out_shape=jax.ShapeDtypeStruct((n_rows, D_MODEL), F32),
        grid_spec=grid_spec,
        input_output_aliases={2: 0},
        compiler_params=pltpu.CompilerParams(dimension_semantics=("arbitrary",), vmem_limit_bytes=VMEM_LIMIT,
                                             has_side_effects=True),
        name="dispatch_rows",
    )(dest, x1, jnp.zeros((n_rows, D_MODEL), F32))


def _ffn_kernel(be_ref, xs_ref, wg_ref, wu_ref, wd_ref, ys_ref):
    del be_ref
    xb = xs_ref[...].astype(BF16)
    hg = _mm(xb, wg_ref[...])
    hu = _mm(xb, wu_ref[...])
    h = (hg * _sigmoid(hg) * hu).astype(BF16)
    ys_ref[...] = _mm(h, wd_ref[...])


def expert_ffn(xs, block_e, wg, wu, wd):
    n_rows = xs.shape[0]
    grid_spec = pltpu.PrefetchScalarGridSpec(
        num_scalar_prefetch=1,
        grid=(n_rows // MOE_BLOCK,),
        in_specs=[pl.BlockSpec((MOE_BLOCK, D_MODEL), lambda i, be: (i, 0)),
                  pl.BlockSpec((None, D_MODEL, D_EXPERT), lambda i, be: (be[i], 0, 0)),
                  pl.BlockSpec((None, D_MODEL, D_EXPERT), lambda i, be: (be[i], 0, 0)),
                  pl.BlockSpec((None, D_EXPERT, D_MODEL), lambda i, be: (be[i], 0, 0))],
        out_specs=pl.BlockSpec((MOE_BLOCK, D_MODEL), lambda i, be: (i, 0)),
    )
    return pl.pallas_call(
        _ffn_kernel,
        out_shape=jax.ShapeDtypeStruct((n_rows, D_MODEL), F32),
        grid_spec=grid_spec,
        compiler_params=pltpu.CompilerParams(dimension_semantics=("arbitrary",), vmem_limit_bytes=VMEM_LIMIT),
        name="expert_ffn",
    )(block_e, xs, wg, wu, wd)


def _combine_kernel(d_ref, x1_ref, rt_ref, g_ref, b_ref, ys_hbm, o_ref, buf, sem, *, tm):
    i = pl.program_id(0)
    n = pl.num_programs(0)
    slot = i % GATHER_SLOTS

    def gather(tile, slot_):
        base = tile * (tm * TOP_K)
        for r in range(tm):
            for j in range(TOP_K):
                _row_copy(ys_hbm, buf.at[slot_, j], sem.at[slot_], d_ref[base + r * TOP_K + j], r).start()

    def wait_tile(slot_):
        for j in range(TOP_K):
            pltpu.make_async_copy(ys_hbm.at[pl.ds(0, tm)], buf.at[slot_, j], sem.at[slot_]).wait()

    @pl.when(i == 0)
    def _():
        for j in range(GATHER_SLOTS - 1):
            gather(jnp.minimum(j, n - 1), j)

    ahead = i + (GATHER_SLOTS - 1)
    gather(jnp.minimum(ahead, n - 1), ahead % GATHER_SLOTS)
    wait_tile(slot)

    rt = rt_ref[...]
    ffn = rt[:, 2:3] * buf[slot, 0] + rt[:, 3:4] * buf[slot, 1]
    o_ref[...] = _layer_norm(DEEP_ALPHA * x1_ref[...] + ffn, g_ref[...], b_ref[...])

    @pl.when(i == n - 1)
    def _():
        for j in range(1, GATHER_SLOTS):
            wait_tile((i + j) % GATHER_SLOTS)


def combine_ln(x1, route, dest, ys, ln_g, ln_b):
    t = x1.shape[0]
    tm = min(TM, t)
    grid_spec = pltpu.PrefetchScalarGridSpec(
        num_scalar_prefetch=1,
        grid=(t // tm,),
        in_specs=[pl.BlockSpec((tm, D_MODEL), lambda i, d: (i, 0)),
                  pl.BlockSpec((tm, LANES), lambda i, d: (i, 0)),
                  pl.BlockSpec((1, D_MODEL), lambda i, d: (0, 0)),
                  pl.BlockSpec((1, D_MODEL), lambda i, d: (0, 0)),
                  pl.BlockSpec(memory_space=pl.ANY)],
        out_specs=pl.BlockSpec((tm, D_MODEL), lambda i, d: (i, 0)),
        scratch_shapes=[pltpu.VMEM((GATHER_SLOTS, TOP_K, tm, D_MODEL), F32),
                        pltpu.SemaphoreType.DMA((GATHER_SLOTS,))],
    )
    return pl.pallas_call(
        functools.partial(_combine_kernel, tm=tm),
        out_shape=jax.ShapeDtypeStruct((t, D_MODEL), F32),
        grid_spec=grid_spec,
        compiler_params=pltpu.CompilerParams(dimension_semantics=("arbitrary",), vmem_limit_bytes=VMEM_LIMIT),
        name="combine_ln",
    )(dest, x1, route, ln_g.reshape(1, D_MODEL), ln_b.reshape(1, D_MODEL), ys)


def route_tables(route, counts):
    t = route.shape[0]
    n_blocks = -(-(t * TOP_K) // MOE_BLOCK) + N_EXP
    counts = counts.astype(jnp.int32)
    padded = (counts + MOE_BLOCK - 1) // MOE_BLOCK * MOE_BLOCK
    pad_end = jnp.cumsum(padded)
    pad_start = pad_end - padded
    flat_e = route[:, :TOP_K].astype(jnp.int32)
    dest = (pad_start[flat_e] + route[:, 4:4 + TOP_K].astype(jnp.int32)).reshape(-1)
    block_e = jnp.minimum(jnp.searchsorted(pad_end, jnp.arange(n_blocks, dtype=jnp.int32) * MOE_BLOCK, side='right'),
                          N_EXP - 1).astype(jnp.int32)
    return block_e, dest, n_blocks * MOE_BLOCK


def _prep_w_in(w_in):
    wa = w_in[:, :ZA_W]
    gi = w_in[:, ZA_W:ZA_W + H_A]
    gf = w_in[:, ZA_W + H_A:N_A_COLS]
    wb = _pad_b_cols(w_in[:, N_A_COLS:N_A_COLS + N_B_COLS])
    wc = w_in[:, N_A_COLS + N_B_COLS:]
    gpad = jnp.zeros((D_MODEL, LANES - H_A), w_in.dtype)
    return jnp.concatenate([wa, wb, gi, gpad, gf, gpad, wc], axis=1).astype(BF16)


def _trunk_layer(x3, st, npad, lc, p):
    b, lp, _ = x3.shape
    c0, n0, m0, cv0, s0, sh0, hr0, hi0 = st
    x2d = x3.reshape(b * lp, D_MODEL)
    z3 = in_proj(x2d, p['w_in'], lp, npad).reshape(b, lp, Z_N)
    ya, st_a = mlstm_mixer(z3, c0, n0, m0, cv0, p['mlstm_conv_w'], p['mlstm_gate_b'], p['mlstm_norm_g'],
                           min(CHUNK_MLSTM, lp), npad)
    yb, st_b = rwkv_mixer(z3, s0, sh0, p['rwkv_mu'], p['rwkv_vecs'], p['rwkv_w_up'], p['rwkv_a_up'], p['rwkv_g_up'], lc)
    yc, st_c = s5_mixer(z3, hr0, hi0, p['s5_A_re'], p['s5_A_im'], p['s5_log_dt'], p['s5_B_re'], p['s5_B_im'],
                        p['s5_C_re'], p['s5_C_im'], p['s5_D'], p['s5_w_glu'], lc)
    t = b * lp
    x1, route, counts = out_proj_router(x2d, ya.reshape(t, D_A), yb.reshape(t, D_B), yc.reshape(t, D_C), p['w_out'],
                                        p['ln_g'][0], p['ln_b'][0], p['moe_w_coarse'], p['moe_b_coarse'],
                                        p['moe_w_fine'], p['moe_b_fine'])
    block_e, dest, n_rows = route_tables(route, counts)
    xs = dispatch_rows(x1, dest, n_rows)
    ys = expert_ffn(xs, block_e, p['moe_w_gate'], p['moe_w_up'], p['moe_w_down'])
    x2 = combine_ln(x1, route, dest, ys, p['ln_g'][1], p['ln_b'][1])
    return x2.reshape(b, lp, D_MODEL), st_a + st_b + st_c


def _run_trunk(x3, states, npad, lc, params):
    new = [[] for _ in states]
    for l in range(DEPTH):
        p = {name: arr[l] for name, arr in params.items()}
        x3, st_new = _trunk_layer(x3, [s[l].astype(F32) for s in states], npad, lc, p)
        for lst, s in zip(new, st_new):
            lst.append(s)
    return x3, [jnp.stack(lst) for lst in new]


def kernel(x_prompt, x_sample, state_mlstm_C, state_mlstm_n, state_mlstm_m, state_mlstm_conv, state_rwkv_S, state_rwkv_shift, state_s5_re, state_s5_im, meta_tokens, w_in, w_out, mlstm_conv_w, mlstm_gate_b, mlstm_norm_g, rwkv_mu, rwkv_vecs, rwkv_w_up, rwkv_a_up, rwkv_g_up, s5_A_re, s5_A_im, s5_log_dt, s5_B_re, s5_B_im, s5_C_re, s5_C_im, s5_D, s5_w_glu, ln_g, ln_b, moe_w_coarse, moe_b_coarse, moe_w_fine, moe_b_fine, moe_w_gate, moe_w_up, moe_w_down):
    params = dict(w_in=jax.vmap(_prep_w_in)(w_in), w_out=w_out.astype(BF16), mlstm_conv_w=mlstm_conv_w,
                  mlstm_gate_b=mlstm_gate_b, mlstm_norm_g=mlstm_norm_g, rwkv_mu=rwkv_mu, rwkv_vecs=rwkv_vecs,
                  rwkv_w_up=rwkv_w_up, rwkv_a_up=rwkv_a_up, rwkv_g_up=rwkv_g_up, s5_A_re=s5_A_re, s5_A_im=s5_A_im,
                  s5_log_dt=s5_log_dt, s5_B_re=s5_B_re, s5_B_im=s5_B_im, s5_C_re=s5_C_re, s5_C_im=s5_C_im,
                  s5_D=s5_D, s5_w_glu=s5_w_glu, ln_g=ln_g, ln_b=ln_b, moe_w_coarse=moe_w_coarse,
                  moe_b_coarse=moe_b_coarse, moe_w_fine=moe_w_fine, moe_b_fine=moe_b_fine,
                  moe_w_gate=moe_w_gate.astype(BF16), moe_w_up=moe_w_up.astype(BF16),
                  moe_w_down=moe_w_down.astype(BF16))
    state_in = [state_mlstm_C, state_mlstm_n, state_mlstm_m, state_mlstm_conv,
                state_rwkv_S, state_rwkv_shift, state_s5_re, state_s5_im]

    b, seq, _ = x_prompt.shape
    real = N_META + seq
    npad = (-real) % SEQ_PAD_MULT
    meta = jnp.broadcast_to(meta_tokens.astype(x_prompt.dtype)[None], (b, N_META, D_MODEL))
    xp = jnp.concatenate([jnp.zeros((b, npad, D_MODEL), x_prompt.dtype), meta, x_prompt], axis=1)
    fresh = [jnp.zeros((DEPTH, b) + s.shape[2:], F32) for s in state_in]
    yp, pst = _run_trunk(xp, fresh, npad, CHUNK, params)
    y_prompt = yp[:, npad + N_META:]
    pst = [s.astype(r.dtype) for s, r in zip(pst, state_in)]

    y_sample, sst = _run_trunk(x_sample, state_in, 0, CHUNK, params)
    sst = [s.astype(r.dtype) for s, r in zip(sst, state_in)]
    return (y_prompt, y_sample, *pst, *sst)
```

```python
import functools
import math

import jax
import jax.numpy as jnp
from jax import lax
from jax.experimental import pallas as pl
from jax.experimental.pallas import tpu as pltpu

F32 = jnp.float32
BF16 = jnp.bfloat16
HI = lax.Precision.HIGHEST

D_MODEL = 1024
DEPTH = 2
N_META = 16
HEAD_DIM = 64
D_A = 384
H_A = D_A // HEAD_DIM
CONV_W = 4
D_B = 384
H_B = D_B // HEAD_DIM
R_W = 64
R_A = 64
R_G = 128
RWKV_GN_EPS = 64e-5
D_C = 256
C_GROUP = 16
G_C = D_C // C_GROUP
P_C = 64
D_MIX = D_A + D_B + D_C
N_A_COLS = 4 * D_A + 2 * H_A
N_B_COLS = 3 * D_B + R_W + R_A + R_G
N_GROUPS = 4
EXP_PER_GROUP = 8
N_EXP = N_GROUPS * EXP_PER_GROUP
TOP_K = 2
D_EXPERT = 512
MOE_BLOCK = 256
DEEP_ALPHA = (2 * DEPTH) ** 0.25
LN_EPS = 1e-5

LANES = 128
SUBLANES = 8
VMEM_LIMIT = 48 * 1024 * 1024

ZA_W = 4 * D_A
ZB_W = 3 * D_B + 3 * LANES
Z_A0 = 0
Z_B0 = ZA_W
Z_GI0 = Z_B0 + ZB_W
Z_GF0 = Z_GI0 + LANES
Z_C0 = Z_GF0 + LANES
Z_N = Z_C0 + D_C
SEQ_PAD_MULT = 256
CHUNK = 64
STEP_ROWS = 128
GATHER_SLOTS = 3
MIXER_PHASE_ORDER = "RRS" "RM" "RS" "RS" "RS" "RS" "RS" "RS" "RRSM"
BATCH_PER_STEP = 2
TM = 256
NEG = -1e30


def _nt(a, b, precision=None):
    return lax.dot_general(a, b, (((1,), (1,)), ((), ())), precision=precision, preferred_element_type=F32)


def _tn(a, b, precision=None):
    return lax.dot_general(a, b, (((0,), (0,)), ((), ())), precision=precision, preferred_element_type=F32)


def _mm(a, b, precision=None):
    return jnp.dot(a, b, precision=precision, preferred_element_type=F32)


def _bf(x):
    return x.astype(BF16)


def _cumsum_rows(tril_bf16, x):
    hi = _bf(x)
    lo = _bf(x - hi.astype(F32))
    return _mm(tril_bf16, hi) + _mm(tril_bf16, lo)


def _sigmoid(x):
    return 1.0 / (1.0 + jnp.exp(-x))


def _softplus(x):
    return jnp.maximum(x, 0.0) + jnp.log1p(jnp.exp(-jnp.abs(x)))


def _head_indicator(n, scale):
    shift = jnp.int32(int(math.log2(HEAD_DIM)))
    r = lax.shift_right_logical(lax.broadcasted_iota(jnp.int32, (n, n), 0), shift)
    c = lax.shift_right_logical(lax.broadcasted_iota(jnp.int32, (n, n), 1), shift)
    return jnp.where(r == c, scale, 0.0).astype(F32)


def _tril(n, strict=False):
    r = lax.broadcasted_iota(jnp.int32, (n, n), 0)
    c = lax.broadcasted_iota(jnp.int32, (n, n), 1)
    return (r > c) if strict else (r >= c)


def _in_proj_kernel(x_ref, w_ref, z_ref, *, tm, lp, npad):
    z = _mm(x_ref[...].astype(BF16), w_ref[...])
    if npad:
        pos = (pl.program_id(0) * tm) % lp + lax.broadcasted_iota(jnp.int32, (tm, 1), 0)
        z = jnp.where(pos >= npad, z, 0.0)
    z_ref[...] = z


def in_proj(x2d, w_bf16, lp, npad):
    t = x2d.shape[0]
    tm = min(TM, t)
    assert t % tm == 0 and (npad == 0 or lp % tm == 0)
    return pl.pallas_call(
        functools.partial(_in_proj_kernel, tm=tm, lp=lp, npad=npad),
        out_shape=jax.ShapeDtypeStruct((t, Z_N), F32),
        grid=(t // tm,),
        in_specs=[pl.BlockSpec((tm, D_MODEL), lambda i: (i, 0)),
                  pl.BlockSpec((D_MODEL, Z_N), lambda i: (0, 0))],
        out_specs=pl.BlockSpec((tm, Z_N), lambda i: (i, 0)),
        compiler_params=pltpu.CompilerParams(dimension_semantics=("arbitrary",), vmem_limit_bytes=VMEM_LIMIT),
        name="in_proj",
    )(x2d, w_bf16)


def _mlstm_init(c0_ref, n0_ref, m0_ref, cv0_ref, c_ref, n_ref, m_ref, cbuf):
    c_ref[...] = c0_ref[...]
    n_ref[...] = n0_ref[...]
    m_ref[...] = m0_ref[...]
    cbuf[:, 0:SUBLANES, :] = cv0_ref[...]


def _mlstm_compute(zq_ref, zi_ref, zf_ref, cw_ref, bi_ref, bf_ref, ng_ref,
                   y_ref, c_ref, n_ref, m_ref, cv_ref, cbuf, *, lc, npad, bb):
    c = pl.program_id(1)
    causal = _tril(lc)
    tril_b = causal.astype(BF16)
    eye8 = (lax.broadcasted_iota(jnp.int32, (SUBLANES, LANES), 0)
            == lax.broadcasted_iota(jnp.int32, (SUBLANES, LANES), 1)).astype(BF16)
    lane = lax.broadcasted_iota(jnp.int32, (1, LANES), 1)
    sub = lax.broadcasted_iota(jnp.int32, (SUBLANES, 1), 0)
    ind = _head_indicator(D_A, 1.0 / HEAD_DIM).astype(BF16)
    prep = []
    for bi in range(bb):
        zq = zq_ref[bi]
        cbuf[bi, SUBLANES:, :] = zq[:, :2 * D_A]
        acc = zq[:, :2 * D_A] * cw_ref[CONV_W - 1:CONV_W, :]
        for j in range(CONV_W - 1):
            d = CONV_W - 1 - j
            acc = acc + cbuf[bi, SUBLANES - d:SUBLANES - d + lc, :] * cw_ref[j:j + 1, :]
        qk = acc * _sigmoid(acc)
        last_rows = cbuf[bi, lc:lc + SUBLANES, :]
        cbuf[bi, 0:SUBLANES, :] = last_rows
        cv_ref[bi] = last_rows

        logi = zi_ref[bi] + bi_ref[...]
        logf = -_softplus(-(zf_ref[bi] + bf_ref[...]))
        if npad:
            pos = c * lc + lax.broadcasted_iota(jnp.int32, (lc, 1), 0)
            logi = jnp.where(pos >= npad, logi, NEG)
            logf = jnp.where(pos >= npad, logf, 0.0)
        bcum = _cumsum_rows(tril_b, logf)
        gcol = logi - bcum
        g_hi = _bf(gcol)
        g_lo = _bf(gcol - g_hi.astype(F32))
        grow_all = _nt(eye8, g_hi) + _nt(eye8, g_lo)
        prep.append(dict(bcum=bcum, gcol=gcol, grow=grow_all, m_row=m_ref[bi], n_all=n_ref[bi],
                         q=_bf(qk[:, :D_A]), k=_bf(qk[:, D_A:] * (HEAD_DIM ** -0.5)), v=zq[:, 2 * D_A:3 * D_A],
                         o=_sigmoid(zq[:, 3 * D_A:4 * D_A])))
        yield
    chains = [(bi, h) for bi in range(bb) for h in range(H_A)]
    n = range(len(chains))
    hs = lambda name: [prep[bi][name][:, h * HEAD_DIM:(h + 1) * HEAD_DIM] for bi, h in chains]
    col = lambda name: [prep[bi][name][:, h:h + 1] for bi, h in chains]
    c0 = [c_ref[bi, h] for bi, h in chains]
    n0 = [prep[bi]['n_all'][h:h + 1, :] for bi, h in chains]
    q, k, v = hs('q'), hs('k'), hs('v')
    bcol, gcl, m0 = col('bcum'), col('gcol'), col('m_row')
    dm = [jnp.where(causal, bcol[i] + prep[bi]['grow'][h:h + 1, :], NEG) for i, (bi, h) in enumerate(chains)]
    g = [bcol[i] + m0[i] for i in n]
    m = [jnp.maximum(g[i], jnp.max(dm[i], axis=-1, keepdims=True)) for i in n]
    w_inter = [jnp.exp(g[i] - m[i]) for i in n]
    s = [_nt(q[i], k[i]) * jnp.exp(dm[i] - m[i]) for i in n]
    yield
    num =[w_inter[i] * _nt(q[i], _bf(c0[i])) + _mm(_bf(s[i]), _bf(v[i])) for i in n]
    den = [w_inter[i] * jnp.sum(q[i].astype(F32) * n0[i], axis=-1, keepdims=True)
           + jnp.sum(s[i], axis=-1, keepdims=True) for i in n]
    hh = [num[i] / jnp.maximum(jnp.abs(den[i]), jnp.exp(-m[i])) for i in n]
    yield
    m_end =[m[i][lc - 1:lc, :] for i in n]
    w_end = [jnp.exp(bcol[i][lc - 1:lc, :] + gcl[i] - m_end[i]) for i in n]
    dec = [jnp.exp(g[i][lc - 1:lc, :] - m_end[i]) for i in n]
    c1 = [dec[i] * c0[i] + _tn(_bf(v[i] * w_end[i]), k[i]) for i in n]
    n1 = [dec[i] * n0[i] + jnp.sum(k[i].astype(F32) * w_end[i], axis=0, keepdims=True) for i in n]
    for i, (bi, h) in enumerate(chains):
        c_ref[bi, h] = c1[i]
    yield
    for bi in range(bb):
        m_new = prep[bi]['m_row']
        n_new = prep[bi]['n_all']
        for h in range(H_A):
            m_new = jnp.where(lane == h, m_end[bi * H_A + h], m_new)
            n_new = jnp.where(sub == h, n1[bi * H_A + h], n_new)
        m_ref[bi] = m_new
        n_ref[bi] = n_new
        hcat = jnp.concatenate(hh[bi * H_A:(bi + 1) * H_A], axis=1)
        mu = _mm(_bf(hcat), ind)
        dev = hcat - mu
        var = _mm(_bf(dev * dev), ind)
        y_ref[bi, :, 0:D_A] = (prep[bi]['o'] * (dev * lax.rsqrt(var + LN_EPS) * ng_ref[...])).astype(y_ref.dtype)


def _rwkv_compute(zb_ref, mu_ref, vec_ref, wup_ref, aup_ref, gup_ref, y_ref, s_ref, sh_ref, *, off, bb):
    lc = CHUNK
    row = lax.broadcasted_iota(jnp.int32, (lc, 1), 0)
    w0, a0, k_k, k_a, r_k, gn_g = (vec_ref[i:i + 1, :] for i in range(6))
    ones_h = _head_indicator(D_B, 1.0).astype(BF16)
    ind = _head_indicator(D_B, 1.0 / HEAD_DIM).astype(BF16)
    incl = _tril(lc)
    tril_b = incl.astype(BF16)
    r2 = lax.broadcasted_iota(jnp.int32, (lc, 2 * lc), 0)
    c2 = lax.broadcasted_iota(jnp.int32, (lc, 2 * lc), 1)
    c2 = jnp.where(c2 >= lc, c2 - lc, c2)
    strict2 = r2 > c2
    incl2 = r2 >= c2
    zeros_h = jnp.zeros((lc, HEAD_DIM), BF16)
    n_dbl = int(math.log2(lc)) - 1
    prep = []
    for bi in range(bb):
        z = zb_ref[bi, off:off + lc, :]
        zprev = jnp.where(row == 0, sh_ref[bi, SUBLANES - 1:SUBLANES, :], pltpu.roll(z, 1, axis=0))
        zs = z + mu_ref[...] * (zprev - z)
        sh_ref[bi] = z[lc - SUBLANES:lc, :]
        r = zs[:, 0:D_B]
        k = zs[:, D_B:2 * D_B]
        v = zs[:, 2 * D_B:3 * D_B]
        xw = zs[:, 3 * D_B:3 * D_B + LANES]
        xa = zs[:, 3 * D_B + LANES:3 * D_B + 2 * LANES]
        xg = zs[:, 3 * D_B + 2 * LANES:3 * D_B + 3 * LANES]
        w_log = -_softplus(-(w0 + _mm(_bf(jnp.tanh(xw)), wup_ref[...]))) - 0.5
        lw = -jnp.exp(w_log)
        a = _sigmoid(a0 + _mm(_bf(xa), aup_ref[...]))
        g = _mm(_bf(_sigmoid(xg)), gup_ref[...])
        kk = k * k_k
        kk = kk / jnp.maximum(jnp.sqrt(_mm(_bf(kk * kk), ones_h)), 1e-12)
        k2 = k * (1.0 + (a - 1.0) * k_a)
        bv = kk * a
        cum = _cumsum_rows(tril_b, lw)
        c_last = cum[lc - 1:lc, :]
        e_neg = jnp.exp(-cum)
        e_end = jnp.exp(c_last - cum)
        rt = _bf(r * jnp.exp(cum))
        at = _bf(-kk * jnp.exp(cum - lw))
        bt = _bf(bv * e_neg)
        kt = _bf(k2 * e_neg)
        bh = _bf(bv * e_end)
        kh = _bf(k2 * e_end)
        vb = _bf(v)
        prep.append(dict(rt=rt, at=at, bt=bt, kt=kt, bh=bh, kh=kh, vb=vb, p_last=jnp.exp(c_last),
                         tail=(gn_g, _mm(_bf(r * k2 * r_k), ones_h) * v, g)))
        yield
    chains = [(bi, h) for bi in range(bb) for h in range(H_B)]
    hs = lambda name: [prep[bi][name][:, h * HEAD_DIM:(h + 1) * HEAD_DIM] for bi, h in chains]
    s0 = [s_ref[bi, h] for bi, h in chains]
    at, rt, bt, kt, bh, kh, vb, p_last = (hs(n) for n in ('at', 'rt', 'bt', 'kt', 'bh', 'kh', 'vb', 'p_last'))
    n = range(len(chains))
    lhs = [jnp.concatenate([at[i], rt[i]], axis=0) for i in n]
    pair = [_nt(lhs[i], jnp.concatenate([bt[i], kt[i]], axis=0)) for i in n]
    a_mat = [jnp.where(strict2, pair[i][:lc], 0.0) for i in n]
    b_mat = [jnp.where(incl2, pair[i][lc:], 0.0) for i in n]
    ls = [_nt(lhs[i], _bf(s0[i])) for i in n]
    w = [ls[i][:lc] + _mm(_bf(a_mat[i]), jnp.concatenate([zeros_h, vb[i]], axis=0)) for i in n]
    yield
    pw = [a_mat[i][:, :lc] for i in n]
    u = [w[i] + _mm(_bf(pw[i]), _bf(w[i])) for i in n]
    for _ in range(n_dbl):
        yield
        pw = [_mm(_bf(pw[i]), _bf(pw[i])) for i in n]
        u = [u[i] + _mm(_bf(pw[i]), _bf(u[i])) for i in n]
    yield
    uv = [jnp.concatenate([_bf(u[i]), vb[i]], axis=0) for i in n]
    yh = [ls[i][lc:] + _mm(_bf(b_mat[i]), uv[i]) for i in n]
    s1 = [s0[i] * p_last[i] + _tn(uv[i], jnp.concatenate([bh[i], kh[i]], axis=0)) for i in n]
    for i, (bi, h) in enumerate(chains):
        s_ref[bi, h] = s1[i]
    yield
    for bi in range(bb):
        y = jnp.concatenate(yh[bi * H_B:(bi + 1) * H_B], axis=1)
        gn_g_, bonus, g = prep[bi]['tail']
        mu = _mm(_bf(y), ind)
        dev = y - mu
        var = _mm(_bf(dev * dev), ind)
        y_ref[bi, off:off + lc, D_A:D_A + D_B] = (
            (dev * lax.rsqrt(var + RWKV_GN_EPS) * gn_g_ + bonus) * g).astype(y_ref.dtype)


def _pad_b_cols(t):
    o = 3 * D_B
    pad = jnp.zeros(t.shape[:-1] + (LANES - R_W,), t.dtype)
    return jnp.concatenate([t[..., :o + R_W], pad, t[..., o + R_W:o + R_W + R_A], pad, t[..., o + R_W + R_A:]], axis=-1)


def _unpad_b_cols(t):
    o = 3 * D_B
    return jnp.concatenate([t[..., :o + R_W], t[..., o + LANES:o + LANES + R_A], t[..., o + 2 * LANES:]], axis=-1)


def _s5_abar(are_ref, aim_ref, ldt_ref):
    a_re = are_ref[...]
    a_im = aim_ref[...]
    dt = jnp.exp(ldt_ref[...])
    mag = jnp.exp(a_re * dt)
    return a_re, a_im, mag * jnp.cos(a_im * dt), mag * jnp.sin(a_im * dt)


def _s5_init(abar, hr0_ref, hi0_ref, bre_ref, bim_ref, hr_ref, hi_ref, bbr, bbi):
    a_re, a_im, ab_re, ab_im = abar
    hr_ref[...] = hr0_ref[...]
    hi_ref[...] = hi0_ref[...]
    inv = 1.0 / (a_re * a_re + a_im * a_im)
    q_re = ((ab_re - 1.0) * a_re + ab_im * a_im) * inv
    q_im = (ab_im * a_re - (ab_re - 1.0) * a_im) * inv
    bbr[...] = _bf(q_re * bre_ref[...] - q_im * bim_ref[...])
    bbi[...] = _bf(q_re * bim_ref[...] + q_im * bre_ref[...])


def _s5_compute(abar, u_ref, cre_ref, cim_ref, d_ref, wglu_ref, y_ref, hr_ref, hi_ref, bbr, bbi, *, off, bb):
    lc = CHUNK
    _, _, ab_re, ab_im = abar
    u = u_ref[:, off:off + lc, :].reshape(bb * lc, D_C)
    row = lax.broadcasted_iota(jnp.int32, (bb * lc, 1), 0)
    pos = jnp.bitwise_and(row, lc - 1)
    ub = _bf(u)
    x_re = _mm(ub, bbr[...])
    x_im = _mm(ub, bbi[...])
    for bi in range(bb):
        h_re = hr_ref[bi]
        h_im = hi_ref[bi]
        first = row == bi * lc
        x_re = x_re + jnp.where(first, ab_re * h_re - ab_im * h_im, 0.0)
        x_im = x_im + jnp.where(first, ab_re * h_im + ab_im * h_re, 0.0)
    p_re, p_im = ab_re, ab_im
    shift = 1
    while shift < lc:
        yield
        keep = pos >= shift
        s_re = jnp.where(keep, pltpu.roll(x_re, shift, axis=0), 0.0)
        s_im = jnp.where(keep, pltpu.roll(x_im, shift, axis=0), 0.0)
        x_re, x_im = x_re + p_re * s_re - p_im * s_im, x_im + p_re * s_im + p_im * s_re
        p_re, p_im = p_re * p_re - p_im * p_im, 2.0 * p_re * p_im
        shift *= 2
    yield
    for bi in range(bb):
        hr_ref[bi] = x_re[(bi + 1) * lc - 1:(bi + 1) * lc, :]
        hi_ref[bi] = x_im[(bi + 1) * lc - 1:(bi + 1) * lc, :]
    y = _mm(_bf(x_re), cre_ref[...]) - _mm(_bf(x_im), cim_ref[...]) + d_ref[...] * u
    y = 0.5 * y * (1.0 + jnp.tanh(math.sqrt(2.0 / math.pi) * (y + 0.044715 * (y * y * y))))
    y = y * _sigmoid(_mm(_bf(y), wglu_ref[...]))
    y_ref[:, off:off + lc, D_A + D_B:] = y.reshape(bb, lc, D_C).astype(y_ref.dtype)


def _block_diag(t):
    g, r, c = t.shape
    eye = jnp.eye(g, dtype=t.dtype)
    return (t[:, :, None, :] * eye[:, None, :, None]).reshape(g * r, g * c)


def _mixers_kernel(zq_ref, zi_ref, zf_ref, c0_ref, n0_ref, m0_ref, cv0_ref, cw_ref, bi_ref, bf_ref, ng_ref,
                   zb_ref, s0_ref, sh0_ref, mu_ref, vec_ref, wup_ref, aup_ref, gup_ref,
                   u_ref, hr0_ref, hi0_ref, are_ref, aim_ref, ldt_ref, bre_ref, bim_ref, cre_ref, cim_ref,
                   d_ref, wglu_ref,
                   y_ref, c_ref, n_ref, m_ref, cv_ref, s_ref, sh_ref, hr_ref, hi_ref,
                   cbuf, bbr, bbi, *, ls, npad, bb):
    abar = _s5_abar(are_ref, aim_ref, ldt_ref)

    @pl.when(pl.program_id(1) == 0)
    def _():
        _mlstm_init(c0_ref, n0_ref, m0_ref, cv0_ref, c_ref, n_ref, m_ref, cbuf)
        s_ref[...] = s0_ref[...]
        sh_ref[...] = sh0_ref[...]
        _s5_init(abar, hr0_ref, hi0_ref, bre_ref, bim_ref, hr_ref, hi_ref, bbr, bbi)

    def rwkv():
        for off in range(0, ls, CHUNK):
            yield from _rwkv_compute(zb_ref, mu_ref, vec_ref, wup_ref, aup_ref, gup_ref, y_ref, s_ref, sh_ref,
                                     off=off, bb=bb)

    def s5():
        for off in range(0, ls, CHUNK):
            yield from _s5_compute(abar, u_ref, cre_ref, cim_ref, d_ref, wglu_ref, y_ref, hr_ref, hi_ref, bbr, bbi,
                                   off=off, bb=bb)

    gens = dict(R=rwkv(), S=s5(), M=_mlstm_compute(zq_ref, zi_ref, zf_ref, cw_ref, bi_ref, bf_ref, ng_ref, y_ref,
                                                   c_ref, n_ref, m_ref, cv_ref, cbuf, lc=ls, npad=npad, bb=bb))
    for name in MIXER_PHASE_ORDER * (ls // CHUNK):
        next(gens[name], None)
    pending = list(gens.values())
    while pending:
        for gen in list(pending):
            if next(gen, StopIteration) is StopIteration:
                pending.remove(gen)


def mixers(z3, st, p, npad):
    b, lp, _ = z3.shape
    c0, n0, m0, cv0, s0, sh0, hr0, hi0 = st
    ls = min(STEP_ROWS, lp)
    bb = BATCH_PER_STEP
    assert b % bb == 0 and lp % ls == 0 and ls % CHUNK == 0
    gp = G_C * P_C
    n0p = jnp.zeros((b, SUBLANES, HEAD_DIM), F32).at[:, :H_A].set(n0)
    m0p = jnp.zeros((b, 1, LANES), F32).at[:, 0, :H_A].set(m0)
    cv0p = jnp.zeros((b, SUBLANES, 2 * D_A), F32).at[:, SUBLANES - (CONV_W - 1):].set(cv0)
    cwp = jnp.zeros((SUBLANES, 2 * D_A), F32).at[:CONV_W].set(p['mlstm_conv_w'])
    bi = jnp.zeros((1, LANES), F32).at[0, :H_A].set(p['mlstm_gate_b'][0])
    bf = jnp.zeros((1, LANES), F32).at[0, :H_A].set(p['mlstm_gate_b'][1])
    sh0p = jnp.zeros((b, SUBLANES, ZB_W), F32).at[:, SUBLANES - 1:].set(_pad_b_cols(sh0))
    mup = _pad_b_cols(p['rwkv_mu'].reshape(1, N_B_COLS))
    vecp = jnp.zeros((SUBLANES, D_B), F32).at[:6].set(p['rwkv_vecs'])
    wupp = jnp.zeros((LANES, D_B), BF16).at[:R_W].set(p['rwkv_w_up'].astype(BF16))
    aupp = jnp.zeros((LANES, D_B), BF16).at[:R_A].set(p['rwkv_a_up'].astype(BF16))
    flat = lambda t: t.reshape(1, gp).astype(F32)
    ldt = jnp.broadcast_to(p['s5_log_dt'].astype(F32)[:, None], (G_C, P_C)).reshape(1, gp)
    bre = _block_diag(jnp.swapaxes(p['s5_B_re'], 1, 2))
    bim = _block_diag(jnp.swapaxes(p['s5_B_im'], 1, 2))
    cre = _block_diag(jnp.swapaxes(p['s5_C_re'], 1, 2))
    cim = _block_diag(jnp.swapaxes(p['s5_C_im'], 1, 2))

    zcol = lambda w, c0_: pl.BlockSpec((bb, ls, w), lambda i, j: (i, j, c0_ // w))
    st_ = lambda *shape: pl.BlockSpec((bb,) + shape, lambda i, j: (i,) + (0,) * len(shape))
    par = lambda *shape: pl.BlockSpec(shape, lambda i, j: (0,) * len(shape))
    state_specs = (st_(H_A, HEAD_DIM, HEAD_DIM), st_(SUBLANES, HEAD_DIM), st_(1, LANES), st_(SUBLANES, 2 * D_A),
                   st_(H_B, HEAD_DIM, HEAD_DIM), st_(SUBLANES, ZB_W), st_(1, gp), st_(1, gp))
    state_shapes = ((b, H_A, HEAD_DIM, HEAD_DIM), (b, SUBLANES, HEAD_DIM), (b, 1, LANES), (b, SUBLANES, 2 * D_A),
                    (b, H_B, HEAD_DIM, HEAD_DIM), (b, SUBLANES, ZB_W), (b, 1, gp), (b, 1, gp))
    y, c1, n1, m1, cv1, s1, sh1, hr1, hi1 = pl.pallas_call(
        functools.partial(_mixers_kernel, ls=ls, npad=npad, bb=bb),
        out_shape=(jax.ShapeDtypeStruct((b, lp, D_MIX), BF16),) + tuple(
            jax.ShapeDtypeStruct(s, F32) for s in state_shapes),
        grid=(b // bb, lp // ls),
        in_specs=[zcol(ZA_W, Z_A0), zcol(LANES, Z_GI0), zcol(LANES, Z_GF0), *state_specs[:4],
                  par(SUBLANES, 2 * D_A), par(1, LANES), par(1, LANES), par(1, D_A),
                  zcol(ZB_W, Z_B0), *state_specs[4:6],
                  par(1, ZB_W), par(SUBLANES, D_B), par(LANES, D_B), par(LANES, D_B), par(R_G, D_B),
                  zcol(D_C, Z_C0), *state_specs[6:], par(1, gp), par(1, gp), par(1, gp),
                  par(D_C, gp), par(D_C, gp), par(gp, D_C), par(gp, D_C), par(1, D_C), par(D_C, D_C)],
        out_specs=(pl.BlockSpec((bb, ls, D_MIX), lambda i, j: (i, j, 0)),) + state_specs,
        scratch_shapes=[pltpu.VMEM((bb, ls + SUBLANES, 2 * D_A), F32),
                        pltpu.VMEM((D_C, gp), BF16), pltpu.VMEM((D_C, gp), BF16)],
        compiler_params=pltpu.CompilerParams(dimension_semantics=("arbitrary", "arbitrary"),
                                             vmem_limit_bytes=VMEM_LIMIT),
        name="mixers",
    )(z3, z3, z3, c0, n0p, m0p, cv0p, cwp, bi, bf, p['mlstm_norm_g'].reshape(1, D_A),
      z3, s0, sh0p, mup, vecp, wupp, aupp, p['rwkv_g_up'].astype(BF16),
      z3, hr0.reshape(b, 1, gp), hi0.reshape(b, 1, gp), flat(p['s5_A_re']), flat(p['s5_A_im']), ldt,
      bre, bim, cre.astype(BF16), cim.astype(BF16), p['s5_D'].reshape(1, D_C), p['s5_w_glu'].astype(BF16))
    states = (c1, n1[:, :H_A], m1[:, 0, :H_A], cv1[:, SUBLANES - (CONV_W - 1):],
              s1, _unpad_b_cols(sh1[:, SUBLANES - 1:]), hr1.reshape(b, G_C, P_C), hi1.reshape(b, G_C, P_C))
    return y, states


def _layer_norm(x, g, b):
    mu = jnp.mean(x, axis=-1, keepdims=True)
    dev = x - mu
    var = jnp.mean(dev * dev, axis=-1, keepdims=True)
    return dev * lax.rsqrt(var + LN_EPS) * g + b


def _argmax_lane(x, lane):
    m = jnp.max(x, axis=-1, keepdims=True)
    idx = jnp.min(jnp.where(x == m, lane, LANES), axis=-1, keepdims=True)
    return m, idx


def _out_proj_kernel(x_ref, y_ref, w_ref, g_ref, b_ref, wr_hi_ref, wr_lo_ref, br_ref, x1_ref, rt_ref, cnt_ref):
    @pl.when(pl.program_id(0) == 0)
    def _():
        cnt_ref[...] = jnp.zeros_like(cnt_ref)

    mix = _mm(y_ref[...], w_ref[...])
    x1 = _layer_norm(DEEP_ALPHA * x_ref[...] + mix, g_ref[...], b_ref[...])
    x1_ref[...] = x1
    tm = x1.shape[0]
    lane = lax.broadcasted_iota(jnp.int32, (tm, LANES), 1)
    x_hi = _bf(x1)
    x_lo = _bf(x1 - x_hi.astype(F32))
    logit = (_mm(x_hi, wr_hi_ref[...]) + _mm(x_hi, wr_lo_ref[...]) + _mm(x_lo, wr_hi_ref[...])) + br_ref[...]
    lco = jnp.where(lane < N_GROUPS, logit, NEG)
    mco, grp = _argmax_lane(lco, lane)
    p_grp = 1.0 / jnp.sum(jnp.exp(lco - mco), axis=-1, keepdims=True)
    lo_lane = N_GROUPS + grp * EXP_PER_GROUP
    lfi = jnp.where((lane >= lo_lane) & (lane < lo_lane + EXP_PER_GROUP), logit, NEG)
    m0, l0 = _argmax_lane(lfi, lane)
    m1, l1 = _argmax_lane(jnp.where(lane == l0, NEG, lfi), lane)
    t1 = jnp.exp(m1 - m0)
    g0 = p_grp / (1.0 + t1)
    g1 = p_grp * t1 / (1.0 + t1)
    onehot = jnp.where((lane == l0) | (lane == l1), 1.0, 0.0)
    before = _mm(_tril(tm, strict=True).astype(BF16), _bf(onehot)) + cnt_ref[...]
    rank0 = jnp.sum(jnp.where(lane == l0, before, 0.0), axis=-1, keepdims=True)
    rank1 = jnp.sum(jnp.where(lane == l1, before, 0.0), axis=-1, keepdims=True)
    cnt_ref[...] += jnp.sum(onehot, axis=0, keepdims=True)
    vals = (l0 - N_GROUPS).astype(F32), (l1 - N_GROUPS).astype(F32), g0, g1, rank0, rank1
    out = jnp.zeros((tm, LANES), F32)
    for j, val in enumerate(vals):
        out = jnp.where(lane == j, val, out)
    rt_ref[...] = out


def out_proj_router(x2d, ymix, w_out_bf16, ln_g, ln_b, w_coarse, b_coarse, w_fine, b_fine):
    t = x2d.shape[0]
    tm = min(TM, t)
    wr = jnp.concatenate([w_coarse, w_fine, jnp.zeros((D_MODEL, LANES - N_GROUPS - N_EXP), F32)], axis=1)
    wr_hi = wr.astype(BF16)
    wr_lo = (wr - wr_hi.astype(F32)).astype(BF16)
    br = jnp.concatenate([b_coarse, b_fine, jnp.zeros((LANES - N_GROUPS - N_EXP,), F32)]).reshape(1, LANES)
    tok = lambda w: pl.BlockSpec((tm, w), lambda i: (i, 0))
    par = lambda *shape: pl.BlockSpec(shape, lambda i: (0,) * len(shape))
    x1, route, cnt = pl.pallas_call(
        _out_proj_kernel,
        out_shape=(jax.ShapeDtypeStruct((t, D_MODEL), F32), jax.ShapeDtypeStruct((t, LANES), F32),
                   jax.ShapeDtypeStruct((1, LANES), F32)),
        grid=(t // tm,),
        in_specs=[tok(D_MODEL), tok(D_MIX), par(D_MIX, D_MODEL), par(1, D_MODEL), par(1, D_MODEL),
                  par(D_MODEL, LANES), par(D_MODEL, LANES), par(1, LANES)],
        out_specs=(tok(D_MODEL), tok(LANES), par(1, LANES)),
        compiler_params=pltpu.CompilerParams(dimension_semantics=("arbitrary",), vmem_limit_bytes=VMEM_LIMIT),
        name="out_proj_router",
    )(x2d, ymix, w_out_bf16, ln_g.reshape(1, D_MODEL), ln_b.reshape(1, D_MODEL), wr_hi, wr_lo, br)
    return x1, route, cnt[0, N_GROUPS:N_GROUPS + N_EXP]


def _row_copy(src_hbm, dst, sem, tok, r):
    return pltpu.make_async_copy(src_hbm.at[pl.ds(tok, 1)], dst.at[pl.ds(r, 1)], sem)


def _dispatch_kernel(d_ref, x1_ref, xs_in, xs_out, sem, *, tm):
    del xs_in
    base = pl.program_id(0) * (tm * TOP_K)
    for r in range(tm):
        for j in range(TOP_K):
            pltpu.make_async_copy(x1_ref.at[pl.ds(r, 1)], xs_out.at[pl.ds(d_ref[base + r * TOP_K + j], 1)], sem).start()
    for j in range(TOP_K):
        pltpu.make_async_copy(x1_ref, xs_out.at[pl.ds(0, tm)], sem).wait()


def dispatch_rows(x1, dest, n_rows):
    t = x1.shape[0]
    tm = min(TM, t)
    grid_spec = pltpu.PrefetchScalarGridSpec(
        num_scalar_prefetch=1,
        grid=(t // tm,),
        in_specs=[pl.BlockSpec((tm, D_MODEL), lambda i, d: (i, 0)), pl.BlockSpec(memory_space=pl.ANY)],
        out_specs=pl.BlockSpec(memory_space=pl.ANY),
        scratch_shapes=[pltpu.SemaphoreType.DMA],
    )
    return pl.pallas_call(
        functools.partial(_dispatch_kernel, tm=tm),
        out_shape=jax.ShapeDtypeStruct((n_rows, D_MODEL), F32),
        grid_spec=grid_spec,
        input_output_aliases={2: 0},
        compiler_params=pltpu.CompilerParams(dimension_semantics=("arbitrary",), vmem_limit_bytes=VMEM_LIMIT,
                                             has_side_effects=True),
        name="dispatch_rows",
    )(dest, x1, jnp.zeros((n_rows, D_MODEL), F32))


def _ffn_kernel(be_ref, xs_ref, wg_ref, wu_ref, wd_ref, ys_ref, wg_b, wu_b, wd_b):
    i = pl.program_id(0)

    @pl.when((i == 0) | (be_ref[i] != be_ref[jnp.maximum(i - 1, 0)]))
    def _():
        wg_b[...] = _bf(wg_ref[...])
        wu_b[...] = _bf(wu_ref[...])
        wd_b[...] = _bf(wd_ref[...])

    xb = xs_ref[...].astype(BF16)
    hg = _mm(xb, wg_b[...])
    hu = _mm(xb, wu_b[...])
    h = (hg * _sigmoid(hg) * hu).astype(BF16)
    ys_ref[...] = _mm(h, wd_b[...])


def expert_ffn(xs, block_e, wg, wu, wd):
    n_rows = xs.shape[0]
    grid_spec = pltpu.PrefetchScalarGridSpec(
        num_scalar_prefetch=1,
        grid=(n_rows // MOE_BLOCK,),
        in_specs=[pl.BlockSpec((MOE_BLOCK, D_MODEL), lambda i, be: (i, 0)),
                  pl.BlockSpec((None, D_MODEL, D_EXPERT), lambda i, be: (be[i], 0, 0)),
                  pl.BlockSpec((None, D_MODEL, D_EXPERT), lambda i, be: (be[i], 0, 0)),
                  pl.BlockSpec((None, D_EXPERT, D_MODEL), lambda i, be: (be[i], 0, 0))],
        out_specs=pl.BlockSpec((MOE_BLOCK, D_MODEL), lambda i, be: (i, 0)),
        scratch_shapes=[pltpu.VMEM((D_MODEL, D_EXPERT), BF16), pltpu.VMEM((D_MODEL, D_EXPERT), BF16),
                        pltpu.VMEM((D_EXPERT, D_MODEL), BF16)],
    )
    return pl.pallas_call(
        _ffn_kernel,
        out_shape=jax.ShapeDtypeStruct((n_rows, D_MODEL), F32),
        grid_spec=grid_spec,
        compiler_params=pltpu.CompilerParams(dimension_semantics=("arbitrary",), vmem_limit_bytes=VMEM_LIMIT),
        name="expert_ffn",
    )(block_e, xs, wg, wu, wd)


def _combine_kernel(d_ref, x1_ref, rt_ref, g_ref, b_ref, ys_hbm, o_ref, buf, sem, *, tm):
    i = pl.program_id(0)
    n = pl.num_programs(0)
    slot = i % GATHER_SLOTS

    def gather(tile, slot_):
        base = tile * (tm * TOP_K)
        for r in range(tm):
            for j in range(TOP_K):
                _row_copy(ys_hbm, buf.at[slot_, j], sem.at[slot_], d_ref[base + r * TOP_K + j], r).start()

    def wait_tile(slot_):
        for j in range(TOP_K):
            pltpu.make_async_copy(ys_hbm.at[pl.ds(0, tm)], buf.at[slot_, j], sem.at[slot_]).wait()

    @pl.when(i == 0)
    def _():
        for j in range(GATHER_SLOTS - 1):
            gather(jnp.minimum(j, n - 1), j)

    ahead = i + (GATHER_SLOTS - 1)
    gather(jnp.minimum(ahead, n - 1), ahead % GATHER_SLOTS)
    wait_tile(slot)

    rt = rt_ref[...]
    ffn = rt[:, 2:3] * buf[slot, 0] + rt[:, 3:4] * buf[slot, 1]
    o_ref[...] = _layer_norm(DEEP_ALPHA * x1_ref[...] + ffn, g_ref[...], b_ref[...])

    @pl.when(i == n - 1)
    def _():
        for j in range(1, GATHER_SLOTS):
            wait_tile((i + j) % GATHER_SLOTS)


def combine_ln(x1, route, dest, ys, ln_g, ln_b):
    t = x1.shape[0]
    tm = min(TM, t)
    grid_spec = pltpu.PrefetchScalarGridSpec(
        num_scalar_prefetch=1,
        grid=(t // tm,),
        in_specs=[pl.BlockSpec((tm, D_MODEL), lambda i, d: (i, 0)),
                  pl.BlockSpec((tm, LANES), lambda i, d: (i, 0)),
                  pl.BlockSpec((1, D_MODEL), lambda i, d: (0, 0)),
                  pl.BlockSpec((1, D_MODEL), lambda i, d: (0, 0)),
                  pl.BlockSpec(memory_space=pl.ANY)],
        out_specs=pl.BlockSpec((tm, D_MODEL), lambda i, d: (i, 0)),
        scratch_shapes=[pltpu.VMEM((GATHER_SLOTS, TOP_K, tm, D_MODEL), F32),
                        pltpu.SemaphoreType.DMA((GATHER_SLOTS,))],
    )
    return pl.pallas_call(
        functools.partial(_combine_kernel, tm=tm),
        out_shape=jax.ShapeDtypeStruct((t, D_MODEL), F32),
        grid_spec=grid_spec,
        compiler_params=pltpu.CompilerParams(dimension_semantics=("arbitrary",), vmem_limit_bytes=VMEM_LIMIT),
        name="combine_ln",
    )(dest, x1, route, ln_g.reshape(1, D_MODEL), ln_b.reshape(1, D_MODEL), ys)


def route_tables(route, counts):
    t = route.shape[0]
    n_blocks = -(-(t * TOP_K) // MOE_BLOCK) + N_EXP
    counts = counts.astype(jnp.int32)
    padded = (counts + MOE_BLOCK - 1) // MOE_BLOCK * MOE_BLOCK
    pad_end = jnp.cumsum(padded)
    pad_start = pad_end - padded
    flat_e = route[:, :TOP_K].astype(jnp.int32)
    dest = (pad_start[flat_e] + route[:, 4:4 + TOP_K].astype(jnp.int32)).reshape(-1)
    first_row = jnp.arange(n_blocks, dtype=jnp.int32) * MOE_BLOCK
    block_e = jnp.minimum(jnp.sum((pad_end[None, :] <= first_row[:, None]).astype(jnp.int32), axis=1), N_EXP - 1)
    return block_e, dest, n_blocks * MOE_BLOCK


def _prep_w_in(w_in):
    wa = w_in[:, :ZA_W]
    gi = w_in[:, ZA_W:ZA_W + H_A]
    gf = w_in[:, ZA_W + H_A:N_A_COLS]
    wb = _pad_b_cols(w_in[:, N_A_COLS:N_A_COLS + N_B_COLS])
    wc = w_in[:, N_A_COLS + N_B_COLS:]
    gpad = jnp.zeros((D_MODEL, LANES - H_A), w_in.dtype)
    return jnp.concatenate([wa, wb, gi, gpad, gf, gpad, wc], axis=1).astype(BF16)


def _trunk_layer(x3, st, npad, p):
    b, lp, _ = x3.shape
    x2d = x3.reshape(b * lp, D_MODEL)
    z3 = in_proj(x2d, p['w_in'], lp, npad).reshape(b, lp, Z_N)
    ymix, st_new = mixers(z3, st, p, npad)
    x1, route, counts = out_proj_router(x2d, ymix.reshape(b * lp, D_MIX), p['w_out'],
                                        p['ln_g'][0], p['ln_b'][0], p['moe_w_coarse'], p['moe_b_coarse'],
                                        p['moe_w_fine'], p['moe_b_fine'])
    block_e, dest, n_rows = route_tables(route, counts)
    xs = dispatch_rows(x1, dest, n_rows)
    ys = expert_ffn(xs, block_e, p['moe_w_gate'], p['moe_w_up'], p['moe_w_down'])
    x2 = combine_ln(x1, route, dest, ys, p['ln_g'][1], p['ln_b'][1])
    return x2.reshape(b, lp, D_MODEL), st_new


def _run_trunk(x3, states, npad, params):
    new = [[] for _ in states]
    for l in range(DEPTH):
        p = {name: arr[l] for name, arr in params.items()}
        x3, st_new = _trunk_layer(x3, [s[l].astype(F32) for s in states], npad, p)
        for lst, s in zip(new, st_new):
            lst.append(s)
    return x3, [jnp.stack(lst) for lst in new]


def kernel(x_prompt, x_sample, state_mlstm_C, state_mlstm_n, state_mlstm_m, state_mlstm_conv, state_rwkv_S, state_rwkv_shift, state_s5_re, state_s5_im, meta_tokens, w_in, w_out, mlstm_conv_w, mlstm_gate_b, mlstm_norm_g, rwkv_mu, rwkv_vecs, rwkv_w_up, rwkv_a_up, rwkv_g_up, s5_A_re, s5_A_im, s5_log_dt, s5_B_re, s5_B_im, s5_C_re, s5_C_im, s5_D, s5_w_glu, ln_g, ln_b, moe_w_coarse, moe_b_coarse, moe_w_fine, moe_b_fine, moe_w_gate, moe_w_up, moe_w_down):
    params = dict(w_in=jax.vmap(_prep_w_in)(w_in), w_out=w_out.astype(BF16), mlstm_conv_w=mlstm_conv_w,
                  mlstm_gate_b=mlstm_gate_b, mlstm_norm_g=mlstm_norm_g, rwkv_mu=rwkv_mu, rwkv_vecs=rwkv_vecs,
                  rwkv_w_up=rwkv_w_up, rwkv_a_up=rwkv_a_up, rwkv_g_up=rwkv_g_up, s5_A_re=s5_A_re, s5_A_im=s5_A_im,
                  s5_log_dt=s5_log_dt, s5_B_re=s5_B_re, s5_B_im=s5_B_im, s5_C_re=s5_C_re, s5_C_im=s5_C_im,
                  s5_D=s5_D, s5_w_glu=s5_w_glu, ln_g=ln_g, ln_b=ln_b, moe_w_coarse=moe_w_coarse,
                  moe_b_coarse=moe_b_coarse, moe_w_fine=moe_w_fine, moe_b_fine=moe_b_fine,
                  moe_w_gate=moe_w_gate, moe_w_up=moe_w_up, moe_w_down=moe_w_down)
    state_in = [state_mlstm_C, state_mlstm_n, state_mlstm_m, state_mlstm_conv,
                state_rwkv_S, state_rwkv_shift, state_s5_re, state_s5_im]

    b, seq, _ = x_prompt.shape
    real = N_META + seq
    npad = (-real) % SEQ_PAD_MULT
    meta = jnp.broadcast_to(meta_tokens.astype(x_prompt.dtype)[None], (b, N_META, D_MODEL))
    xp = jnp.concatenate([jnp.zeros((b, npad, D_MODEL), x_prompt.dtype), meta, x_prompt], axis=1)
    fresh = [jnp.zeros((DEPTH, b) + s.shape[2:], F32) for s in state_in]
    yp, pst = _run_trunk(xp, fresh, npad, params)
    y_prompt = yp[:, npad + N_META:]
    pst = [s.astype(r.dtype) for s, r in zip(pst, state_in)]

    y_sample, sst = _run_trunk(x_sample, state_in, 0, params)
    sst = [s.astype(r.dtype) for s, r in zip(sst, state_in)]
    return (y_prompt, y_sample, *pst, *sst)
```

```python
import functools
import math

import jax
import jax.numpy as jnp
from jax import lax
from jax.experimental import pallas as pl
from jax.experimental.pallas import tpu as pltpu

F32 = jnp.float32
BF16 = jnp.bfloat16
HI = lax.Precision.HIGHEST

D_MODEL = 1024
DEPTH = 2
N_META = 16
HEAD_DIM = 64
D_A = 384
H_A = D_A // HEAD_DIM
CONV_W = 4
D_B = 384
H_B = D_B // HEAD_DIM
R_W = 64
R_A = 64
R_G = 128
RWKV_GN_EPS = 64e-5
D_C = 256
C_GROUP = 16
G_C = D_C // C_GROUP
P_C = 64
D_MIX = D_A + D_B + D_C
N_A_COLS = 4 * D_A + 2 * H_A
N_B_COLS = 3 * D_B + R_W + R_A + R_G
N_GROUPS = 4
EXP_PER_GROUP = 8
N_EXP = N_GROUPS * EXP_PER_GROUP
TOP_K = 2
D_EXPERT = 512
MOE_BLOCK = 256
DEEP_ALPHA = (2 * DEPTH) ** 0.25
LN_EPS = 1e-5

LANES = 128
SUBLANES = 8
VMEM_LIMIT = 48 * 1024 * 1024

ZA_W = 4 * D_A
ZB_W = 3 * D_B + 3 * LANES
Z_A0 = 0
Z_B0 = ZA_W
Z_GI0 = Z_B0 + ZB_W
Z_GF0 = Z_GI0 + LANES
Z_C0 = Z_GF0 + LANES
Z_N = Z_C0 + D_C
SEQ_PAD_MULT = 256
CHUNK = 64
STEP_ROWS = 128
GATHER_SLOTS = 3
MIXER_PHASE_ORDER = "RRS" "RM" "RS" "RS" "RS" "RS" "RS" "RS" "RRSM"
BATCH_PER_STEP = 2
TM = 256
NEG = -1e30


def _nt(a, b, precision=None):
    return lax.dot_general(a, b, (((1,), (1,)), ((), ())), precision=precision, preferred_element_type=F32)


def _tn(a, b, precision=None):
    return lax.dot_general(a, b, (((0,), (0,)), ((), ())), precision=precision, preferred_element_type=F32)


def _mm(a, b, precision=None):
    return jnp.dot(a, b, precision=precision, preferred_element_type=F32)


def _bf(x):
    return x.astype(BF16)


def _cumsum_rows(tril_bf16, x):
    hi = _bf(x)
    lo = _bf(x - hi.astype(F32))
    return _mm(tril_bf16, hi) + _mm(tril_bf16, lo)


def _sigmoid(x):
    return 1.0 / (1.0 + jnp.exp(-x))


def _softplus(x):
    return jnp.maximum(x, 0.0) + jnp.log1p(jnp.exp(-jnp.abs(x)))


def _head_indicator(n, scale):
    shift = jnp.int32(int(math.log2(HEAD_DIM)))
    r = lax.shift_right_logical(lax.broadcasted_iota(jnp.int32, (n, n), 0), shift)
    c = lax.shift_right_logical(lax.broadcasted_iota(jnp.int32, (n, n), 1), shift)
    return jnp.where(r == c, scale, 0.0).astype(F32)


def _tril(n, strict=False):
    r = lax.broadcasted_iota(jnp.int32, (n, n), 0)
    c = lax.broadcasted_iota(jnp.int32, (n, n), 1)
    return (r > c) if strict else (r >= c)


def _in_proj_kernel(x_ref, w_ref, z_ref, *, tm, lp, npad):
    z = _mm(x_ref[...].astype(BF16), w_ref[...])
    if npad:
        pos = (pl.program_id(0) * tm) % lp + lax.broadcasted_iota(jnp.int32, (tm, 1), 0)
        z = jnp.where(pos >= npad, z, 0.0)
    z_ref[...] = z


def in_proj(x2d, w_bf16, lp, npad):
    t = x2d.shape[0]
    tm = min(TM, t)
    assert t % tm == 0 and (npad == 0 or lp % tm == 0)
    return pl.pallas_call(
        functools.partial(_in_proj_kernel, tm=tm, lp=lp, npad=npad),
        out_shape=jax.ShapeDtypeStruct((t, Z_N), F32),
        grid=(t // tm,),
        in_specs=[pl.BlockSpec((tm, D_MODEL), lambda i: (i, 0)),
                  pl.BlockSpec((D_MODEL, Z_N), lambda i: (0, 0))],
        out_specs=pl.BlockSpec((tm, Z_N), lambda i: (i, 0)),
        compiler_params=pltpu.CompilerParams(dimension_semantics=("arbitrary",), vmem_limit_bytes=VMEM_LIMIT),
        name="in_proj",
    )(x2d, w_bf16)


def _mlstm_init(c0_ref, n0_ref, m0_ref, cv0_ref, c_ref, n_ref, m_ref, cbuf):
    c_ref[...] = c0_ref[...]
    n_ref[...] = n0_ref[...]
    m_ref[...] = m0_ref[...]
    cbuf[:, 0:SUBLANES, :] = cv0_ref[...]


def _mlstm_compute(zq_ref, zi_ref, zf_ref, cw_ref, bi_ref, bf_ref, ng_ref,
                   y_ref, c_ref, n_ref, m_ref, cv_ref, cbuf, *, lc, npad, bb):
    c = pl.program_id(1)
    causal = _tril(lc)
    tril_b = causal.astype(BF16)
    eye8 = (lax.broadcasted_iota(jnp.int32, (SUBLANES, LANES), 0)
            == lax.broadcasted_iota(jnp.int32, (SUBLANES, LANES), 1)).astype(BF16)
    lane = lax.broadcasted_iota(jnp.int32, (1, LANES), 1)
    sub = lax.broadcasted_iota(jnp.int32, (SUBLANES, 1), 0)
    ind = _head_indicator(D_A, 1.0 / HEAD_DIM).astype(BF16)
    prep = []
    for bi in range(bb):
        zq = zq_ref[bi]
        cbuf[bi, SUBLANES:, :] = zq[:, :2 * D_A]
        acc = zq[:, :2 * D_A] * cw_ref[CONV_W - 1:CONV_W, :]
        for j in range(CONV_W - 1):
            d = CONV_W - 1 - j
            acc = acc + cbuf[bi, SUBLANES - d:SUBLANES - d + lc, :] * cw_ref[j:j + 1, :]
        qk = acc * _sigmoid(acc)
        last_rows = cbuf[bi, lc:lc + SUBLANES, :]
        cbuf[bi, 0:SUBLANES, :] = last_rows
        cv_ref[bi] = last_rows

        logi = zi_ref[bi] + bi_ref[...]
        logf = -_softplus(-(zf_ref[bi] + bf_ref[...]))
        if npad:
            pos = c * lc + lax.broadcasted_iota(jnp.int32, (lc, 1), 0)
            logi = jnp.where(pos >= npad, logi, NEG)
            logf = jnp.where(pos >= npad, logf, 0.0)
        bcum = _cumsum_rows(tril_b, logf)
        gcol = logi - bcum
        g_hi = _bf(gcol)
        g_lo = _bf(gcol - g_hi.astype(F32))
        grow_all = _nt(eye8, g_hi) + _nt(eye8, g_lo)
        prep.append(dict(bcum=bcum, gcol=gcol, grow=grow_all, m_row=m_ref[bi], n_all=n_ref[bi],
                         q=_bf(qk[:, :D_A]), k=_bf(qk[:, D_A:] * (HEAD_DIM ** -0.5)), v=zq[:, 2 * D_A:3 * D_A],
                         o=_sigmoid(zq[:, 3 * D_A:4 * D_A])))
        yield
    chains = [(bi, h) for bi in range(bb) for h in range(H_A)]
    n = range(len(chains))
    hs = lambda name: [prep[bi][name][:, h * HEAD_DIM:(h + 1) * HEAD_DIM] for bi, h in chains]
    col = lambda name: [prep[bi][name][:, h:h + 1] for bi, h in chains]
    c0 = [c_ref[bi, h] for bi, h in chains]
    n0 = [prep[bi]['n_all'][h:h + 1, :] for bi, h in chains]
    q, k, v = hs('q'), hs('k'), hs('v')
    bcol, gcl, m0 = col('bcum'), col('gcol'), col('m_row')
    dm = [jnp.where(causal, bcol[i] + prep[bi]['grow'][h:h + 1, :], NEG) for i, (bi, h) in enumerate(chains)]
    g = [bcol[i] + m0[i] for i in n]
    m = [jnp.maximum(g[i], jnp.max(dm[i], axis=-1, keepdims=True)) for i in n]
    w_inter = [jnp.exp(g[i] - m[i]) for i in n]
    s = [_nt(q[i], k[i]) * jnp.exp(dm[i] - m[i]) for i in n]
    yield
    num =[w_inter[i] * _nt(q[i], _bf(c0[i])) + _mm(_bf(s[i]), _bf(v[i])) for i in n]
    den = [w_inter[i] * jnp.sum(q[i].astype(F32) * n0[i], axis=-1, keepdims=True)
           + jnp.sum(s[i], axis=-1, keepdims=True) for i in n]
    hh = [num[i] / jnp.maximum(jnp.abs(den[i]), jnp.exp(-m[i])) for i in n]
    yield
    m_end =[m[i][lc - 1:lc, :] for i in n]
    w_end = [jnp.exp(bcol[i][lc - 1:lc, :] + gcl[i] - m_end[i]) for i in n]
    dec = [jnp.exp(g[i][lc - 1:lc, :] - m_end[i]) for i in n]
    c1 = [dec[i] * c0[i] + _tn(_bf(v[i] * w_end[i]), k[i]) for i in n]
    n1 = [dec[i] * n0[i] + jnp.sum(k[i].astype(F32) * w_end[i], axis=0, keepdims=True) for i in n]
    for i, (bi, h) in enumerate(chains):
        c_ref[bi, h] = c1[i]
    yield
    for bi in range(bb):
        m_new = prep[bi]['m_row']
        n_new = prep[bi]['n_all']
        for h in range(H_A):
            m_new = jnp.where(lane == h, m_end[bi * H_A + h], m_new)
            n_new = jnp.where(sub == h, n1[bi * H_A + h], n_new)
        m_ref[bi] = m_new
        n_ref[bi] = n_new
        hcat = jnp.concatenate(hh[bi * H_A:(bi + 1) * H_A], axis=1)
        mu = _mm(_bf(hcat), ind)
        dev = hcat - mu
        var = _mm(_bf(dev * dev), ind)
        y_ref[bi, :, 0:D_A] = (prep[bi]['o'] * (dev * lax.rsqrt(var + LN_EPS) * ng_ref[...])).astype(y_ref.dtype)


def _rwkv_compute(zb_ref, mu_ref, vec_ref, wup_ref, aup_ref, gup_ref, y_ref, s_ref, sh_ref, *, off, bb):
    lc = CHUNK
    row = lax.broadcasted_iota(jnp.int32, (lc, 1), 0)
    w0, a0, k_k, k_a, r_k, gn_g = (vec_ref[i:i + 1, :] for i in range(6))
    ones_h = _head_indicator(D_B, 1.0).astype(BF16)
    ind = _head_indicator(D_B, 1.0 / HEAD_DIM).astype(BF16)
    incl = _tril(lc)
    tril_b = incl.astype(BF16)
    r2 = lax.broadcasted_iota(jnp.int32, (lc, 2 * lc), 0)
    c2 = lax.broadcasted_iota(jnp.int32, (lc, 2 * lc), 1)
    c2 = jnp.where(c2 >= lc, c2 - lc, c2)
    strict2 = r2 > c2
    incl2 = r2 >= c2
    zeros_h = jnp.zeros((lc, HEAD_DIM), BF16)
    n_dbl = int(math.log2(lc)) - 1
    prep = []
    for bi in range(bb):
        z = zb_ref[bi, off:off + lc, :]
        zprev = jnp.where(row == 0, sh_ref[bi, SUBLANES - 1:SUBLANES, :], pltpu.roll(z, 1, axis=0))
        zs = z + mu_ref[...] * (zprev - z)
        sh_ref[bi] = z[lc - SUBLANES:lc, :]
        r = zs[:, 0:D_B]
        k = zs[:, D_B:2 * D_B]
        v = zs[:, 2 * D_B:3 * D_B]
        xw = zs[:, 3 * D_B:3 * D_B + LANES]
        xa = zs[:, 3 * D_B + LANES:3 * D_B + 2 * LANES]
        xg = zs[:, 3 * D_B + 2 * LANES:3 * D_B + 3 * LANES]
        w_log = -_softplus(-(w0 + _mm(_bf(jnp.tanh(xw)), wup_ref[...]))) - 0.5
        lw = -jnp.exp(w_log)
        a = _sigmoid(a0 + _mm(_bf(xa), aup_ref[...]))
        g = _mm(_bf(_sigmoid(xg)), gup_ref[...])
        kk = k * k_k
        kk = kk * lax.rsqrt(jnp.maximum(_mm(_bf(kk * kk), ones_h), 1e-24))
        k2 = k * (1.0 + (a - 1.0) * k_a)
        bv = kk * a
        cum = _cumsum_rows(tril_b, lw)
        c_last = cum[lc - 1:lc, :]
        e_neg = jnp.exp(-cum)
        e_end = jnp.exp(c_last - cum)
        rt = _bf(r * jnp.exp(cum))
        at = _bf(-kk * jnp.exp(cum - lw))
        bt = _bf(bv * e_neg)
        kt = _bf(k2 * e_neg)
        bh = _bf(bv * e_end)
        kh = _bf(k2 * e_end)
        vb = _bf(v)
        prep.append(dict(rt=rt, at=at, bt=bt, kt=kt, bh=bh, kh=kh, vb=vb, p_last=jnp.exp(c_last),
                         tail=(gn_g, _mm(_bf(r * k2 * r_k), ones_h) * v, g)))
        yield
    chains = [(bi, h) for bi in range(bb) for h in range(H_B)]
    hs = lambda name: [prep[bi][name][:, h * HEAD_DIM:(h + 1) * HEAD_DIM] for bi, h in chains]
    s0 = [s_ref[bi, h] for bi, h in chains]
    at, rt, bt, kt, bh, kh, vb, p_last = (hs(n) for n in ('at', 'rt', 'bt', 'kt', 'bh', 'kh', 'vb', 'p_last'))
    n = range(len(chains))
    lhs = [jnp.concatenate([at[i], rt[i]], axis=0) for i in n]
    pair = [_nt(lhs[i], jnp.concatenate([bt[i], kt[i]], axis=0)) for i in n]
    a_mat = [jnp.where(strict2, pair[i][:lc], 0.0) for i in n]
    b_mat = [jnp.where(incl2, pair[i][lc:], 0.0) for i in n]
    ls = [_nt(lhs[i], _bf(s0[i])) for i in n]
    w = [ls[i][:lc] + _mm(_bf(a_mat[i]), jnp.concatenate([zeros_h, vb[i]], axis=0)) for i in n]
    yield
    pw = [a_mat[i][:, :lc] for i in n]
    u = [w[i] + _mm(_bf(pw[i]), _bf(w[i])) for i in n]
    for _ in range(n_dbl):
        yield
        pw = [_mm(_bf(pw[i]), _bf(pw[i])) for i in n]
        u = [u[i] + _mm(_bf(pw[i]), _bf(u[i])) for i in n]
    yield
    uv = [jnp.concatenate([_bf(u[i]), vb[i]], axis=0) for i in n]
    yh = [ls[i][lc:] + _mm(_bf(b_mat[i]), uv[i]) for i in n]
    s1 = [s0[i] * p_last[i] + _tn(uv[i], jnp.concatenate([bh[i], kh[i]], axis=0)) for i in n]
    for i, (bi, h) in enumerate(chains):
        s_ref[bi, h] = s1[i]
    yield
    for bi in range(bb):
        y = jnp.concatenate(yh[bi * H_B:(bi + 1) * H_B], axis=1)
        gn_g_, bonus, g = prep[bi]['tail']
        mu = _mm(_bf(y), ind)
        dev = y - mu
        var = _mm(_bf(dev * dev), ind)
        y_ref[bi, off:off + lc, D_A:D_A + D_B] = (
            (dev * lax.rsqrt(var + RWKV_GN_EPS) * gn_g_ + bonus) * g).astype(y_ref.dtype)


def _pad_b_cols(t):
    o = 3 * D_B
    pad = jnp.zeros(t.shape[:-1] + (LANES - R_W,), t.dtype)
    return jnp.concatenate([t[..., :o + R_W], pad, t[..., o + R_W:o + R_W + R_A], pad, t[..., o + R_W + R_A:]], axis=-1)


def _unpad_b_cols(t):
    o = 3 * D_B
    return jnp.concatenate([t[..., :o + R_W], t[..., o + LANES:o + LANES + R_A], t[..., o + 2 * LANES:]], axis=-1)


def _s5_abar(are_ref, aim_ref, ldt_ref):
    a_re = are_ref[...]
    a_im = aim_ref[...]
    dt = jnp.exp(ldt_ref[...])
    mag = jnp.exp(a_re * dt)
    return a_re, a_im, mag * jnp.cos(a_im * dt), mag * jnp.sin(a_im * dt)


def _s5_init(abar, hr0_ref, hi0_ref, bre_ref, bim_ref, hr_ref, hi_ref, bbr, bbi):
    a_re, a_im, ab_re, ab_im = abar
    hr_ref[...] = hr0_ref[...]
    hi_ref[...] = hi0_ref[...]
    inv = 1.0 / (a_re * a_re + a_im * a_im)
    q_re = ((ab_re - 1.0) * a_re + ab_im * a_im) * inv
    q_im = (ab_im * a_re - (ab_re - 1.0) * a_im) * inv
    bbr[...] = _bf(q_re * bre_ref[...] - q_im * bim_ref[...])
    bbi[...] = _bf(q_re * bim_ref[...] + q_im * bre_ref[...])


def _s5_compute(abar, u_ref, cre_ref, cim_ref, d_ref, wglu_ref, y_ref, hr_ref, hi_ref, bbr, bbi, *, off, bb):
    lc = CHUNK
    _, _, ab_re, ab_im = abar
    u = u_ref[:, off:off + lc, :].reshape(bb * lc, D_C)
    ub = _bf(u)
    bu_re = _mm(ub, bbr[...])
    bu_im = _mm(ub, bbi[...])
    row8 = lax.broadcasted_iota(jnp.int32, (SUBLANES, 1), 0)
    pos = lax.broadcasted_iota(jnp.int32, (lc, 1), 0)
    xs = []
    for bi in range(bb):
        h_re = hr_ref[bi]
        h_im = hi_ref[bi]
        x_re = bu_re[bi * lc:(bi + 1) * lc]
        x_im = bu_im[bi * lc:(bi + 1) * lc]
        head_re = x_re[:SUBLANES] + jnp.where(row8 == 0, ab_re * h_re - ab_im * h_im, 0.0)
        head_im = x_im[:SUBLANES] + jnp.where(row8 == 0, ab_re * h_im + ab_im * h_re, 0.0)
        xs.append((jnp.concatenate([head_re, x_re[SUBLANES:]], axis=0),
                   jnp.concatenate([head_im, x_im[SUBLANES:]], axis=0)))
    p_re, p_im = ab_re, ab_im
    shift = 1
    while shift < lc:
        yield
        nxt = []
        for x_re, x_im in xs:
            if shift < SUBLANES:
                keep = pos >= shift
                s_re = jnp.where(keep, pltpu.roll(x_re, shift, axis=0), 0.0)
                s_im = jnp.where(keep, pltpu.roll(x_im, shift, axis=0), 0.0)
                nxt.append((x_re + p_re * s_re - p_im * s_im, x_im + p_re * s_im + p_im * s_re))
            else:
                s_re, s_im = x_re[:lc - shift], x_im[:lc - shift]
                nxt.append((jnp.concatenate([x_re[:shift], x_re[shift:] + p_re * s_re - p_im * s_im], axis=0),
                            jnp.concatenate([x_im[:shift], x_im[shift:] + p_re * s_im + p_im * s_re], axis=0)))
        xs = nxt
        p_re, p_im = p_re * p_re - p_im * p_im, 2.0 * p_re * p_im
        shift *= 2
    yield
    for bi in range(bb):
        hr_ref[bi] = xs[bi][0][lc - 1:lc, :]
        hi_ref[bi] = xs[bi][1][lc - 1:lc, :]
    x_re = jnp.concatenate([x[0] for x in xs], axis=0)
    x_im = jnp.concatenate([x[1] for x in xs], axis=0)
    y = _mm(_bf(x_re), cre_ref[...]) - _mm(_bf(x_im), cim_ref[...]) + d_ref[...] * u
    y = 0.5 * y * (1.0 + jnp.tanh(math.sqrt(2.0 / math.pi) * (y + 0.044715 * (y * y * y))))
    y = y * _sigmoid(_mm(_bf(y), wglu_ref[...]))
    y_ref[:, off:off + lc, D_A + D_B:] = y.reshape(bb, lc, D_C).astype(y_ref.dtype)


def _block_diag(t):
    g, r, c = t.shape
    eye = jnp.eye(g, dtype=t.dtype)
    return (t[:, :, None, :] * eye[:, None, :, None]).reshape(g * r, g * c)


def _mixers_kernel(zq_ref, zi_ref, zf_ref, c0_ref, n0_ref, m0_ref, cv0_ref, cw_ref, bi_ref, bf_ref, ng_ref,
                   zb_ref, s0_ref, sh0_ref, mu_ref, vec_ref, wup_ref, aup_ref, gup_ref,
                   u_ref, hr0_ref, hi0_ref, are_ref, aim_ref, ldt_ref, bre_ref, bim_ref, cre_ref, cim_ref,
                   d_ref, wglu_ref,
                   y_ref, c_ref, n_ref, m_ref, cv_ref, s_ref, sh_ref, hr_ref, hi_ref,
                   cbuf, bbr, bbi, *, ls, npad, bb):
    abar = _s5_abar(are_ref, aim_ref, ldt_ref)

    @pl.when(pl.program_id(1) == 0)
    def _():
        _mlstm_init(c0_ref, n0_ref, m0_ref, cv0_ref, c_ref, n_ref, m_ref, cbuf)
        s_ref[...] = s0_ref[...]
        sh_ref[...] = sh0_ref[...]
        _s5_init(abar, hr0_ref, hi0_ref, bre_ref, bim_ref, hr_ref, hi_ref, bbr, bbi)

    def rwkv():
        for off in range(0, ls, CHUNK):
            yield from _rwkv_compute(zb_ref, mu_ref, vec_ref, wup_ref, aup_ref, gup_ref, y_ref, s_ref, sh_ref,
                                     off=off, bb=bb)

    def s5():
        for off in range(0, ls, CHUNK):
            yield from _s5_compute(abar, u_ref, cre_ref, cim_ref, d_ref, wglu_ref, y_ref, hr_ref, hi_ref, bbr, bbi,
                                   off=off, bb=bb)

    gens = dict(R=rwkv(), S=s5(), M=_mlstm_compute(zq_ref, zi_ref, zf_ref, cw_ref, bi_ref, bf_ref, ng_ref, y_ref,
                                                   c_ref, n_ref, m_ref, cv_ref, cbuf, lc=ls, npad=npad, bb=bb))
    for name in MIXER_PHASE_ORDER * (ls // CHUNK):
        next(gens[name], None)
    pending = list(gens.values())
    while pending:
        for gen in list(pending):
            if next(gen, StopIteration) is StopIteration:
                pending.remove(gen)


def mixers(z3, st, p, npad):
    b, lp, _ = z3.shape
    c0, n0, m0, cv0, s0, sh0, hr0, hi0 = st
    ls = min(STEP_ROWS, lp)
    bb = BATCH_PER_STEP
    assert b % bb == 0 and lp % ls == 0 and ls % CHUNK == 0
    gp = G_C * P_C
    n0p = jnp.zeros((b, SUBLANES, HEAD_DIM), F32).at[:, :H_A].set(n0)
    m0p = jnp.zeros((b, 1, LANES), F32).at[:, 0, :H_A].set(m0)
    cv0p = jnp.zeros((b, SUBLANES, 2 * D_A), F32).at[:, SUBLANES - (CONV_W - 1):].set(cv0)
    cwp = jnp.zeros((SUBLANES, 2 * D_A), F32).at[:CONV_W].set(p['mlstm_conv_w'])
    bi = jnp.zeros((1, LANES), F32).at[0, :H_A].set(p['mlstm_gate_b'][0])
    bf = jnp.zeros((1, LANES), F32).at[0, :H_A].set(p['mlstm_gate_b'][1])
    sh0p = jnp.zeros((b, SUBLANES, ZB_W), F32).at[:, SUBLANES - 1:].set(_pad_b_cols(sh0))
    mup = _pad_b_cols(p['rwkv_mu'].reshape(1, N_B_COLS))
    vecp = jnp.zeros((SUBLANES, D_B), F32).at[:6].set(p['rwkv_vecs'])
    wupp = jnp.zeros((LANES, D_B), BF16).at[:R_W].set(p['rwkv_w_up'].astype(BF16))
    aupp = jnp.zeros((LANES, D_B), BF16).at[:R_A].set(p['rwkv_a_up'].astype(BF16))
    flat = lambda t: t.reshape(1, gp).astype(F32)
    ldt = jnp.broadcast_to(p['s5_log_dt'].astype(F32)[:, None], (G_C, P_C)).reshape(1, gp)
    bre = _block_diag(jnp.swapaxes(p['s5_B_re'], 1, 2))
    bim = _block_diag(jnp.swapaxes(p['s5_B_im'], 1, 2))
    cre = _block_diag(jnp.swapaxes(p['s5_C_re'], 1, 2))
    cim = _block_diag(jnp.swapaxes(p['s5_C_im'], 1, 2))

    zcol = lambda w, c0_: pl.BlockSpec((bb, ls, w), lambda i, j: (i, j, c0_ // w))
    st_ = lambda *shape: pl.BlockSpec((bb,) + shape, lambda i, j: (i,) + (0,) * len(shape))
    par = lambda *shape: pl.BlockSpec(shape, lambda i, j: (0,) * len(shape))
    state_specs = (st_(H_A, HEAD_DIM, HEAD_DIM), st_(SUBLANES, HEAD_DIM), st_(1, LANES), st_(SUBLANES, 2 * D_A),
                   st_(H_B, HEAD_DIM, HEAD_DIM), st_(SUBLANES, ZB_W), st_(1, gp), st_(1, gp))
    state_shapes = ((b, H_A, HEAD_DIM, HEAD_DIM), (b, SUBLANES, HEAD_DIM), (b, 1, LANES), (b, SUBLANES, 2 * D_A),
                    (b, H_B, HEAD_DIM, HEAD_DIM), (b, SUBLANES, ZB_W), (b, 1, gp), (b, 1, gp))
    y, c1, n1, m1, cv1, s1, sh1, hr1, hi1 = pl.pallas_call(
        functools.partial(_mixers_kernel, ls=ls, npad=npad, bb=bb),
        out_shape=(jax.ShapeDtypeStruct((b, lp, D_MIX), BF16),) + tuple(
            jax.ShapeDtypeStruct(s, F32) for s in state_shapes),
        grid=(b // bb, lp // ls),
        in_specs=[zcol(ZA_W, Z_A0), zcol(LANES, Z_GI0), zcol(LANES, Z_GF0), *state_specs[:4],
                  par(SUBLANES, 2 * D_A), par(1, LANES), par(1, LANES), par(1, D_A),
                  zcol(ZB_W, Z_B0), *state_specs[4:6],
                  par(1, ZB_W), par(SUBLANES, D_B), par(LANES, D_B), par(LANES, D_B), par(R_G, D_B),
                  zcol(D_C, Z_C0), *state_specs[6:], par(1, gp), par(1, gp), par(1, gp),
                  par(D_C, gp), par(D_C, gp), par(gp, D_C), par(gp, D_C), par(1, D_C), par(D_C, D_C)],
        out_specs=(pl.BlockSpec((bb, ls, D_MIX), lambda i, j: (i, j, 0)),) + state_specs,
        scratch_shapes=[pltpu.VMEM((bb, ls + SUBLANES, 2 * D_A), F32),
                        pltpu.VMEM((D_C, gp), BF16), pltpu.VMEM((D_C, gp), BF16)],
        compiler_params=pltpu.CompilerParams(dimension_semantics=("arbitrary", "arbitrary"),
                                             vmem_limit_bytes=VMEM_LIMIT),
        name="mixers",
    )(z3, z3, z3, c0, n0p, m0p, cv0p, cwp, bi, bf, p['mlstm_norm_g'].reshape(1, D_A),
      z3, s0, sh0p, mup, vecp, wupp, aupp, p['rwkv_g_up'].astype(BF16),
      z3, hr0.reshape(b, 1, gp), hi0.reshape(b, 1, gp), flat(p['s5_A_re']), flat(p['s5_A_im']), ldt,
      bre, bim, cre.astype(BF16), cim.astype(BF16), p['s5_D'].reshape(1, D_C), p['s5_w_glu'].astype(BF16))
    states = (c1, n1[:, :H_A], m1[:, 0, :H_A], cv1[:, SUBLANES - (CONV_W - 1):],
              s1, _unpad_b_cols(sh1[:, SUBLANES - 1:]), hr1.reshape(b, G_C, P_C), hi1.reshape(b, G_C, P_C))
    return y, states


def _layer_norm(x, g, b):
    mu = jnp.mean(x, axis=-1, keepdims=True)
    dev = x - mu
    var = jnp.mean(dev * dev, axis=-1, keepdims=True)
    return dev * lax.rsqrt(var + LN_EPS) * g + b


def _argmax_lane(x, lane):
    m = jnp.max(x, axis=-1, keepdims=True)
    idx = jnp.min(jnp.where(x == m, lane, LANES), axis=-1, keepdims=True)
    return m, idx


def _out_proj_kernel(x_ref, y_ref, w_ref, g_ref, b_ref, wr_hi_ref, wr_lo_ref, br_ref, x1_ref, rt_ref, cnt_ref):
    @pl.when(pl.program_id(0) == 0)
    def _():
        cnt_ref[...] = jnp.zeros_like(cnt_ref)

    mix = _mm(y_ref[...], w_ref[...])
    x1 = _layer_norm(DEEP_ALPHA * x_ref[...] + mix, g_ref[...], b_ref[...])
    x1_ref[...] = x1
    tm = x1.shape[0]
    lane = lax.broadcasted_iota(jnp.int32, (tm, LANES), 1)
    x_hi = _bf(x1)
    x_lo = _bf(x1 - x_hi.astype(F32))
    logit = (_mm(x_hi, wr_hi_ref[...]) + _mm(x_hi, wr_lo_ref[...]) + _mm(x_lo, wr_hi_ref[...])) + br_ref[...]
    lco = jnp.where(lane < N_GROUPS, logit, NEG)
    mco, grp = _argmax_lane(lco, lane)
    p_grp = 1.0 / jnp.sum(jnp.exp(lco - mco), axis=-1, keepdims=True)
    lo_lane = N_GROUPS + grp * EXP_PER_GROUP
    lfi = jnp.where((lane >= lo_lane) & (lane < lo_lane + EXP_PER_GROUP), logit, NEG)
    m0, l0 = _argmax_lane(lfi, lane)
    m1, l1 = _argmax_lane(jnp.where(lane == l0, NEG, lfi), lane)
    t1 = jnp.exp(m1 - m0)
    g0 = p_grp / (1.0 + t1)
    g1 = p_grp * t1 / (1.0 + t1)
    onehot = jnp.where((lane == l0) | (lane == l1), 1.0, 0.0)
    before = _mm(_tril(tm, strict=True).astype(BF16), _bf(onehot)) + cnt_ref[...]
    rank0 = jnp.sum(jnp.where(lane == l0, before, 0.0), axis=-1, keepdims=True)
    rank1 = jnp.sum(jnp.where(lane == l1, before, 0.0), axis=-1, keepdims=True)
    cnt_ref[...] += jnp.sum(onehot, axis=0, keepdims=True)
    vals = (l0 - N_GROUPS).astype(F32), (l1 - N_GROUPS).astype(F32), g0, g1, rank0, rank1
    out = jnp.zeros((tm, LANES), F32)
    for j, val in enumerate(vals):
        out = jnp.where(lane == j, val, out)
    rt_ref[...] = out


def out_proj_router(x2d, ymix, w_out_bf16, ln_g, ln_b, w_coarse, b_coarse, w_fine, b_fine):
    t = x2d.shape[0]
    tm = min(TM, t)
    wr = jnp.concatenate([w_coarse, w_fine, jnp.zeros((D_MODEL, LANES - N_GROUPS - N_EXP), F32)], axis=1)
    wr_hi = wr.astype(BF16)
    wr_lo = (wr - wr_hi.astype(F32)).astype(BF16)
    br = jnp.concatenate([b_coarse, b_fine, jnp.zeros((LANES - N_GROUPS - N_EXP,), F32)]).reshape(1, LANES)
    tok = lambda w: pl.BlockSpec((tm, w), lambda i: (i, 0))
    par = lambda *shape: pl.BlockSpec(shape, lambda i: (0,) * len(shape))
    x1, route, cnt = pl.pallas_call(
        _out_proj_kernel,
        out_shape=(jax.ShapeDtypeStruct((t, D_MODEL), F32), jax.ShapeDtypeStruct((t, LANES), F32),
                   jax.ShapeDtypeStruct((1, LANES), F32)),
        grid=(t // tm,),
        in_specs=[tok(D_MODEL), tok(D_MIX), par(D_MIX, D_MODEL), par(1, D_MODEL), par(1, D_MODEL),
                  par(D_MODEL, LANES), par(D_MODEL, LANES), par(1, LANES)],
        out_specs=(tok(D_MODEL), tok(LANES), par(1, LANES)),
        compiler_params=pltpu.CompilerParams(dimension_semantics=("arbitrary",), vmem_limit_bytes=VMEM_LIMIT),
        name="out_proj_router",
    )(x2d, ymix, w_out_bf16, ln_g.reshape(1, D_MODEL), ln_b.reshape(1, D_MODEL), wr_hi, wr_lo, br)
    return x1, route, cnt[0, N_GROUPS:N_GROUPS + N_EXP]


def _row_copy(src_hbm, dst, sem, tok, r):
    return pltpu.make_async_copy(src_hbm.at[pl.ds(tok, 1)], dst.at[pl.ds(r, 1)], sem)


def _dispatch_kernel(d_ref, zb_ref, x1_ref, xs_out, zbuf, sem, zsem, *, tm):
    @pl.when(pl.program_id(0) == 0)
    def _():
        zbuf[...] = jnp.zeros_like(zbuf)
        for k in range(zb_ref.shape[0]):
            @pl.when(zb_ref[k] >= 0)
            def _():
                cp = pltpu.make_async_copy(zbuf, xs_out.at[pl.ds(zb_ref[k] * MOE_BLOCK, MOE_BLOCK)], zsem)
                cp.start()
                cp.wait()

    base = pl.program_id(0) * (tm * TOP_K)
    for r in range(tm):
        for j in range(TOP_K):
            pltpu.make_async_copy(x1_ref.at[pl.ds(r, 1)], xs_out.at[pl.ds(d_ref[base + r * TOP_K + j], 1)], sem).start()
    for j in range(TOP_K):
        pltpu.make_async_copy(x1_ref, xs_out.at[pl.ds(0, tm)], sem).wait()


def dispatch_rows(x1, dest, zero_blocks, n_rows):
    t = x1.shape[0]
    tm = min(TM, t)
    grid_spec = pltpu.PrefetchScalarGridSpec(
        num_scalar_prefetch=2,
        grid=(t // tm,),
        in_specs=[pl.BlockSpec((tm, D_MODEL), lambda i, d, zb: (i, 0))],
        out_specs=pl.BlockSpec(memory_space=pl.ANY),
        scratch_shapes=[pltpu.VMEM((MOE_BLOCK, D_MODEL), F32), pltpu.SemaphoreType.DMA, pltpu.SemaphoreType.DMA],
    )
    return pl.pallas_call(
        functools.partial(_dispatch_kernel, tm=tm),
        out_shape=jax.ShapeDtypeStruct((n_rows, D_MODEL), F32),
        grid_spec=grid_spec,
        compiler_params=pltpu.CompilerParams(dimension_semantics=("arbitrary",), vmem_limit_bytes=VMEM_LIMIT,
                                             has_side_effects=True),
        name="dispatch_rows",
    )(dest, zero_blocks, x1)


def _ffn_kernel(be_ref, xs_ref, wg_ref, wu_ref, wd_ref, ys_ref, wg_b, wu_b, wd_b):
    i = pl.program_id(0)

    @pl.when((i == 0) | (be_ref[i] != be_ref[jnp.maximum(i - 1, 0)]))
    def _():
        wg_b[...] = _bf(wg_ref[...])
        wu_b[...] = _bf(wu_ref[...])
        wd_b[...] = _bf(wd_ref[...])

    xb = xs_ref[...].astype(BF16)
    hg = _mm(xb, wg_b[...])
    hu = _mm(xb, wu_b[...])
    h = (hg * _sigmoid(hg) * hu).astype(BF16)
    ys_ref[...] = _mm(h, wd_b[...])


def expert_ffn(xs, block_e, wg, wu, wd, layer):
    n_rows = xs.shape[0]
    grid_spec = pltpu.PrefetchScalarGridSpec(
        num_scalar_prefetch=1,
        grid=(n_rows // MOE_BLOCK,),
        in_specs=[pl.BlockSpec((MOE_BLOCK, D_MODEL), lambda i, be: (i, 0)),
                  pl.BlockSpec((None, None, D_MODEL, D_EXPERT), lambda i, be: (layer, be[i], 0, 0)),
                  pl.BlockSpec((None, None, D_MODEL, D_EXPERT), lambda i, be: (layer, be[i], 0, 0)),
                  pl.BlockSpec((None, None, D_EXPERT, D_MODEL), lambda i, be: (layer, be[i], 0, 0))],
        out_specs=pl.BlockSpec((MOE_BLOCK, D_MODEL), lambda i, be: (i, 0)),
        scratch_shapes=[pltpu.VMEM((D_MODEL, D_EXPERT), BF16), pltpu.VMEM((D_MODEL, D_EXPERT), BF16),
                        pltpu.VMEM((D_EXPERT, D_MODEL), BF16)],
    )
    return pl.pallas_call(
        _ffn_kernel,
        out_shape=jax.ShapeDtypeStruct((n_rows, D_MODEL), F32),
        grid_spec=grid_spec,
        compiler_params=pltpu.CompilerParams(dimension_semantics=("arbitrary",), vmem_limit_bytes=VMEM_LIMIT),
        name="expert_ffn",
    )(block_e, xs, wg, wu, wd)


def _combine_kernel(d_ref, x1_ref, rt_ref, g_ref, b_ref, ys_hbm, o_ref, buf, sem, *, tm):
    i = pl.program_id(0)
    n = pl.num_programs(0)
    slot = i % GATHER_SLOTS

    def gather(tile, slot_):
        base = tile * (tm * TOP_K)
        for r in range(tm):
            for j in range(TOP_K):
                _row_copy(ys_hbm, buf.at[slot_, j], sem.at[slot_], d_ref[base + r * TOP_K + j], r).start()

    def wait_tile(slot_):
        for j in range(TOP_K):
            pltpu.make_async_copy(ys_hbm.at[pl.ds(0, tm)], buf.at[slot_, j], sem.at[slot_]).wait()

    @pl.when(i == 0)
    def _():
        for j in range(GATHER_SLOTS - 1):
            gather(jnp.minimum(j, n - 1), j)

    ahead = i + (GATHER_SLOTS - 1)
    gather(jnp.minimum(ahead, n - 1), ahead % GATHER_SLOTS)
    wait_tile(slot)

    rt = rt_ref[...]
    ffn = rt[:, 2:3] * buf[slot, 0] + rt[:, 3:4] * buf[slot, 1]
    o_ref[...] = _layer_norm(DEEP_ALPHA * x1_ref[...] + ffn, g_ref[...], b_ref[...])

    @pl.when(i == n - 1)
    def _():
        for j in range(1, GATHER_SLOTS):
            wait_tile((i + j) % GATHER_SLOTS)


def combine_ln(x1, route, dest, ys, ln_g, ln_b):
    t = x1.shape[0]
    tm = min(TM, t)
    grid_spec = pltpu.PrefetchScalarGridSpec(
        num_scalar_prefetch=1,
        grid=(t // tm,),
        in_specs=[pl.BlockSpec((tm, D_MODEL), lambda i, d: (i, 0)),
                  pl.BlockSpec((tm, LANES), lambda i, d: (i, 0)),
                  pl.BlockSpec((1, D_MODEL), lambda i, d: (0, 0)),
                  pl.BlockSpec((1, D_MODEL), lambda i, d: (0, 0)),
                  pl.BlockSpec(memory_space=pl.ANY)],
        out_specs=pl.BlockSpec((tm, D_MODEL), lambda i, d: (i, 0)),
        scratch_shapes=[pltpu.VMEM((GATHER_SLOTS, TOP_K, tm, D_MODEL), F32),
                        pltpu.SemaphoreType.DMA((GATHER_SLOTS,))],
    )
    return pl.pallas_call(
        functools.partial(_combine_kernel, tm=tm),
        out_shape=jax.ShapeDtypeStruct((t, D_MODEL), F32),
        grid_spec=grid_spec,
        compiler_params=pltpu.CompilerParams(dimension_semantics=("arbitrary",), vmem_limit_bytes=VMEM_LIMIT),
        name="combine_ln",
    )(dest, x1, route, ln_g.reshape(1, D_MODEL), ln_b.reshape(1, D_MODEL), ys)


def route_tables(route, counts):
    t = route.shape[0]
    n_blocks = -(-(t * TOP_K) // MOE_BLOCK) + N_EXP
    counts = counts.astype(jnp.int32)
    padded = (counts + MOE_BLOCK - 1) // MOE_BLOCK * MOE_BLOCK
    pad_end = jnp.cumsum(padded)
    pad_start = pad_end - padded
    flat_e = route[:, :TOP_K].astype(jnp.int32).reshape(-1)
    dest = pad_start[flat_e] + route[:, 4:4 + TOP_K].astype(jnp.int32).reshape(-1)
    first_row = jnp.arange(n_blocks, dtype=jnp.int32) * MOE_BLOCK
    block_e = jnp.minimum(jnp.sum((pad_end[None, :] <= first_row[:, None]).astype(jnp.int32), axis=1), N_EXP - 1)
    last_blk = jnp.where(padded > counts, pad_end // MOE_BLOCK - 1, -1)
    tail_blk = pad_end[-1] // MOE_BLOCK + jnp.arange(N_EXP, dtype=jnp.int32)
    zero_blocks = jnp.concatenate([last_blk, jnp.where(tail_blk < n_blocks, tail_blk, -1)]).astype(jnp.int32)
    return block_e, dest, zero_blocks, n_blocks * MOE_BLOCK


def _prep_w_in(w_in):
    wa = w_in[:, :ZA_W]
    gi = w_in[:, ZA_W:ZA_W + H_A]
    gf = w_in[:, ZA_W + H_A:N_A_COLS]
    wb = _pad_b_cols(w_in[:, N_A_COLS:N_A_COLS + N_B_COLS])
    wc = w_in[:, N_A_COLS + N_B_COLS:]
    gpad = jnp.zeros((D_MODEL, LANES - H_A), w_in.dtype)
    return jnp.concatenate([wa, wb, gi, gpad, gf, gpad, wc], axis=1).astype(BF16)


def _trunk_layer(x3, st, npad, p, experts, layer):
    b, lp, _ = x3.shape
    x2d = x3.reshape(b * lp, D_MODEL)
    z3 = in_proj(x2d, p['w_in'], lp, npad).reshape(b, lp, Z_N)
    ymix, st_new = mixers(z3, st, p, npad)
    x1, route, counts = out_proj_router(x2d, ymix.reshape(b * lp, D_MIX), p['w_out'],
                                        p['ln_g'][0], p['ln_b'][0], p['moe_w_coarse'], p['moe_b_coarse'],
                                        p['moe_w_fine'], p['moe_b_fine'])
    block_e, dest, zero_blocks, n_rows = route_tables(route, counts)
    xs = dispatch_rows(x1, dest, zero_blocks, n_rows)
    ys = expert_ffn(xs, block_e, *experts, layer)
    x2 = combine_ln(x1, route, dest, ys, p['ln_g'][1], p['ln_b'][1])
    return x2.reshape(b, lp, D_MODEL), st_new


def _run_trunk(x3, states, npad, params, experts):
    new = [[] for _ in states]
    for l in range(DEPTH):
        p = {name: arr[l] for name, arr in params.items()}
        x3, st_new = _trunk_layer(x3, [s[l].astype(F32) for s in states], npad, p, experts, l)
        for lst, s in zip(new, st_new):
            lst.append(s)
    return x3, [jnp.stack(lst) for lst in new]


def kernel(x_prompt, x_sample, state_mlstm_C, state_mlstm_n, state_mlstm_m, state_mlstm_conv, state_rwkv_S, state_rwkv_shift, state_s5_re, state_s5_im, meta_tokens, w_in, w_out, mlstm_conv_w, mlstm_gate_b, mlstm_norm_g, rwkv_mu, rwkv_vecs, rwkv_w_up, rwkv_a_up, rwkv_g_up, s5_A_re, s5_A_im, s5_log_dt, s5_B_re, s5_B_im, s5_C_re, s5_C_im, s5_D, s5_w_glu, ln_g, ln_b, moe_w_coarse, moe_b_coarse, moe_w_fine, moe_b_fine, moe_w_gate, moe_w_up, moe_w_down):
    params = dict(w_in=jax.vmap(_prep_w_in)(w_in), w_out=w_out.astype(BF16), mlstm_conv_w=mlstm_conv_w,
                  mlstm_gate_b=mlstm_gate_b, mlstm_norm_g=mlstm_norm_g, rwkv_mu=rwkv_mu, rwkv_vecs=rwkv_vecs,
                  rwkv_w_up=rwkv_w_up, rwkv_a_up=rwkv_a_up, rwkv_g_up=rwkv_g_up, s5_A_re=s5_A_re, s5_A_im=s5_A_im,
                  s5_log_dt=s5_log_dt, s5_B_re=s5_B_re, s5_B_im=s5_B_im, s5_C_re=s5_C_re, s5_C_im=s5_C_im,
                  s5_D=s5_D, s5_w_glu=s5_w_glu, ln_g=ln_g, ln_b=ln_b, moe_w_coarse=moe_w_coarse,
                  moe_b_coarse=moe_b_coarse, moe_w_fine=moe_w_fine, moe_b_fine=moe_b_fine,
                  )
    experts = (moe_w_gate, moe_w_up, moe_w_down)
    state_in = [state_mlstm_C, state_mlstm_n, state_mlstm_m, state_mlstm_conv,
                state_rwkv_S, state_rwkv_shift, state_s5_re, state_s5_im]

    b, seq, _ = x_prompt.shape
    real = N_META + seq
    npad = (-real) % SEQ_PAD_MULT
    meta = jnp.broadcast_to(meta_tokens.astype(x_prompt.dtype)[None], (b, N_META, D_MODEL))
    xp = jnp.concatenate([jnp.zeros((b, npad, D_MODEL), x_prompt.dtype), meta, x_prompt], axis=1)
    fresh = [jnp.zeros((DEPTH, b) + s.shape[2:], F32) for s in state_in]
    yp, pst = _run_trunk(xp, fresh, npad, params, experts)
    y_prompt = yp[:, npad + N_META:]
    pst = [s.astype(r.dtype) for s, r in zip(pst, state_in)]

    y_sample, sst = _run_trunk(x_sample, state_in, 0, params, experts)
    sst = [s.astype(r.dtype) for s, r in zip(sst, state_in)]
    return (y_prompt, y_sample, *pst, *sst)
```

```python
import functools
import math

import jax
import jax.numpy as jnp
from jax import lax
from jax.experimental import pallas as pl
from jax.experimental.pallas import tpu as pltpu

F32 = jnp.float32
BF16 = jnp.bfloat16
HI = lax.Precision.HIGHEST

D_MODEL = 1024
DEPTH = 2
N_META = 16
HEAD_DIM = 64
D_A = 384
H_A = D_A // HEAD_DIM
CONV_W = 4
D_B = 384
H_B = D_B // HEAD_DIM
R_W = 64
R_A = 64
R_G = 128
RWKV_GN_EPS = 64e-5
D_C = 256
C_GROUP = 16
G_C = D_C // C_GROUP
P_C = 64
D_MIX = D_A + D_B + D_C
N_A_COLS = 4 * D_A + 2 * H_A
N_B_COLS = 3 * D_B + R_W + R_A + R_G
N_GROUPS = 4
EXP_PER_GROUP = 8
N_EXP = N_GROUPS * EXP_PER_GROUP
TOP_K = 2
D_EXPERT = 512
MOE_BLOCK = 256
DEEP_ALPHA = (2 * DEPTH) ** 0.25
LN_EPS = 1e-5

LANES = 128
SUBLANES = 8
VMEM_LIMIT = 48 * 1024 * 1024

ZA_W = 4 * D_A
ZB_W = 3 * D_B + 3 * LANES
Z_A0 = 0
Z_B0 = ZA_W
Z_GI0 = Z_B0 + ZB_W
Z_GF0 = Z_GI0 + LANES
Z_C0 = Z_GF0 + LANES
Z_N = Z_C0 + D_C
SEQ_PAD_MULT = 256
CHUNK = 64
STEP_ROWS = 128
GATHER_SLOTS = 3
MIXER_PHASE_ORDER = "RPRPS" "RMP" "RSP" "RS" "RS" "RS" "RS" "RS" "RRSM"
BATCH_PER_STEP = 2
TM = 256
NEG = -1e30


def _nt(a, b, precision=None):
    return lax.dot_general(a, b, (((1,), (1,)), ((), ())), precision=precision, preferred_element_type=F32)


def _tn(a, b, precision=None):
    return lax.dot_general(a, b, (((0,), (0,)), ((), ())), precision=precision, preferred_element_type=F32)


def _mm(a, b, precision=None):
    return jnp.dot(a, b, precision=precision, preferred_element_type=F32)


class _Dots:
    def __init__(self, exact):
        self.dt = F32 if exact else BF16
        self.prec = HI if exact else None

    def cast(self, x):
        return x.astype(self.dt)

    def mm(self, a, b):
        return _mm(a, b, self.prec)

    def nt(self, a, b):
        return _nt(a, b, self.prec)

    def tn(self, a, b):
        return _tn(a, b, self.prec)

    @staticmethod
    def mm_weights(a, w):
        return _mm(a.astype(BF16), w.astype(BF16))


def _bf(x):
    return x.astype(BF16)


def _cumsum_rows(tril_bf16, x):
    hi = _bf(x)
    lo = _bf(x - hi.astype(F32))
    return _mm(tril_bf16, hi) + _mm(tril_bf16, lo)


def _sigmoid(x):
    return 1.0 / (1.0 + jnp.exp(-x))


def _softplus(x):
    return jnp.maximum(x, 0.0) + jnp.log1p(jnp.exp(-jnp.abs(x)))


def _head_indicator(n, scale):
    shift = jnp.int32(int(math.log2(HEAD_DIM)))
    r = lax.shift_right_logical(lax.broadcasted_iota(jnp.int32, (n, n), 0), shift)
    c = lax.shift_right_logical(lax.broadcasted_iota(jnp.int32, (n, n), 1), shift)
    return jnp.where(r == c, scale, 0.0).astype(F32)


def _tril(n, strict=False):
    r = lax.broadcasted_iota(jnp.int32, (n, n), 0)
    c = lax.broadcasted_iota(jnp.int32, (n, n), 1)
    return (r > c) if strict else (r >= c)


def _mlstm_init(c0_ref, n0_ref, m0_ref, cv0_ref, c_ref, n_ref, m_ref, cbuf):
    c_ref[...] = c0_ref[...]
    n_ref[...] = n0_ref[...]
    m_ref[...] = m0_ref[...]
    cbuf[:, 0:SUBLANES, :] = cv0_ref[...]


def _mlstm_compute(z_ref, cw_ref, bi_ref, bf_ref, ng_ref,
                   y_ref, c_ref, n_ref, m_ref, cv_ref, cbuf, *, lc, npad, bb, dots):
    _mm, _nt, _tn, _bf, cdt = dots.mm, dots.nt, dots.tn, dots.cast, dots.dt
    c = pl.program_id(1)
    causal = _tril(lc)
    tril_b = causal.astype(BF16)
    eye8 = (lax.broadcasted_iota(jnp.int32, (SUBLANES, LANES), 0)
            == lax.broadcasted_iota(jnp.int32, (SUBLANES, LANES), 1)).astype(cdt)
    lane = lax.broadcasted_iota(jnp.int32, (1, LANES), 1)
    sub = lax.broadcasted_iota(jnp.int32, (SUBLANES, 1), 0)
    ind = _head_indicator(D_A, 1.0 / HEAD_DIM).astype(cdt)
    prep = []
    for bi in range(bb):
        zq = z_ref[bi, :, Z_A0:Z_A0 + ZA_W]
        cbuf[bi, SUBLANES:, :] = zq[:, :2 * D_A]
        acc = zq[:, :2 * D_A] * cw_ref[CONV_W - 1:CONV_W, :]
        for j in range(CONV_W - 1):
            d = CONV_W - 1 - j
            acc = acc + cbuf[bi, SUBLANES - d:SUBLANES - d + lc, :] * cw_ref[j:j + 1, :]
        qk = acc * _sigmoid(acc)
        last_rows = cbuf[bi, lc:lc + SUBLANES, :]
        cbuf[bi, 0:SUBLANES, :] = last_rows
        cv_ref[bi] = last_rows

        logi = z_ref[bi, :, Z_GI0:Z_GI0 + LANES] + bi_ref[...]
        logf = -_softplus(-(z_ref[bi, :, Z_GF0:Z_GF0 + LANES] + bf_ref[...]))
        if npad:
            pos = c * lc + lax.broadcasted_iota(jnp.int32, (lc, 1), 0)
            logi = jnp.where(pos >= npad, logi, NEG)
            logf = jnp.where(pos >= npad, logf, 0.0)
        bcum = _cumsum_rows(tril_b, logf)
        gcol = logi - bcum
        g_hi = _bf(gcol)
        g_lo = _bf(gcol - g_hi.astype(F32))
        grow_all = _nt(eye8, g_hi) + _nt(eye8, g_lo)
        prep.append(dict(bcum=bcum, gcol=gcol, grow=grow_all, m_row=m_ref[bi], n_all=n_ref[bi],
                         q=_bf(qk[:, :D_A]), k=_bf(qk[:, D_A:] * (HEAD_DIM ** -0.5)), v=zq[:, 2 * D_A:3 * D_A],
                         o=_sigmoid(zq[:, 3 * D_A:4 * D_A])))
        yield
    chains = [(bi, h) for bi in range(bb) for h in range(H_A)]
    n = range(len(chains))
    hs = lambda name: [prep[bi][name][:, h * HEAD_DIM:(h + 1) * HEAD_DIM] for bi, h in chains]
    col = lambda name: [prep[bi][name][:, h:h + 1] for bi, h in chains]
    c0 = [c_ref[bi, h] for bi, h in chains]
    n0 = [prep[bi]['n_all'][h:h + 1, :] for bi, h in chains]
    q, k, v = hs('q'), hs('k'), hs('v')
    bcol, gcl, m0 = col('bcum'), col('gcol'), col('m_row')
    dm = [jnp.where(causal, bcol[i] + prep[bi]['grow'][h:h + 1, :], NEG) for i, (bi, h) in enumerate(chains)]
    g = [bcol[i] + m0[i] for i in n]
    m = [jnp.maximum(g[i], jnp.max(dm[i], axis=-1, keepdims=True)) for i in n]
    w_inter = [jnp.exp(g[i] - m[i]) for i in n]
    s = [_nt(q[i], k[i]) * jnp.exp(dm[i] - m[i]) for i in n]
    yield
    num =[w_inter[i] * _nt(q[i], _bf(c0[i])) + _mm(_bf(s[i]), _bf(v[i])) for i in n]
    den = [w_inter[i] * jnp.sum(q[i].astype(F32) * n0[i], axis=-1, keepdims=True)
           + jnp.sum(s[i], axis=-1, keepdims=True) for i in n]
    hh = [num[i] / jnp.maximum(jnp.abs(den[i]), jnp.exp(-m[i])) for i in n]
    yield
    m_end =[m[i][lc - 1:lc, :] for i in n]
    w_end = [jnp.exp(bcol[i][lc - 1:lc, :] + gcl[i] - m_end[i]) for i in n]
    dec = [jnp.exp(g[i][lc - 1:lc, :] - m_end[i]) for i in n]
    c1 = [dec[i] * c0[i] + _tn(_bf(v[i] * w_end[i]), k[i]) for i in n]
    n1 = [dec[i] * n0[i] + jnp.sum(k[i].astype(F32) * w_end[i], axis=0, keepdims=True) for i in n]
    for i, (bi, h) in enumerate(chains):
        c_ref[bi, h] = c1[i]
    yield
    for bi in range(bb):
        m_new = prep[bi]['m_row']
        n_new = prep[bi]['n_all']
        for h in range(H_A):
            m_new = jnp.where(lane == h, m_end[bi * H_A + h], m_new)
            n_new = jnp.where(sub == h, n1[bi * H_A + h], n_new)
        m_ref[bi] = m_new
        n_ref[bi] = n_new
        hcat = jnp.concatenate(hh[bi * H_A:(bi + 1) * H_A], axis=1)
        mu = _mm(_bf(hcat), ind)
        dev = hcat - mu
        var = _mm(_bf(dev * dev), ind)
        y_ref[bi, :, 0:D_A] = (prep[bi]['o'] * (dev * lax.rsqrt(var + LN_EPS) * ng_ref[...])).astype(y_ref.dtype)


def _rwkv_compute(z_ref, mu_ref, vec_ref, wup_ref, aup_ref, gup_ref, y_ref, s_ref, sh_ref, *, off, bb, dots):
    _mm, _nt, _tn, _bf, cdt = dots.mm, dots.nt, dots.tn, dots.cast, dots.dt
    lc = CHUNK
    row = lax.broadcasted_iota(jnp.int32, (lc, 1), 0)
    w0, a0, k_k, k_a, r_k, gn_g = (vec_ref[i:i + 1, :] for i in range(6))
    ones_h = _head_indicator(D_B, 1.0).astype(cdt)
    ind = _head_indicator(D_B, 1.0 / HEAD_DIM).astype(cdt)
    incl = _tril(lc)
    tril_b = incl.astype(BF16)
    r2 = lax.broadcasted_iota(jnp.int32, (lc, 2 * lc), 0)
    c2 = lax.broadcasted_iota(jnp.int32, (lc, 2 * lc), 1)
    c2 = jnp.where(c2 >= lc, c2 - lc, c2)
    strict2 = r2 > c2
    incl2 = r2 >= c2
    zeros_h = jnp.zeros((lc, HEAD_DIM), cdt)
    n_dbl = int(math.log2(lc)) - 1
    prep = []
    for bi in range(bb):
        z = z_ref[bi, off:off + lc, Z_B0:Z_B0 + ZB_W]
        zprev = jnp.where(row == 0, sh_ref[bi, SUBLANES - 1:SUBLANES, :], pltpu.roll(z, 1, axis=0))
        zs = z + mu_ref[...] * (zprev - z)
        sh_ref[bi] = z[lc - SUBLANES:lc, :]
        r = zs[:, 0:D_B]
        k = zs[:, D_B:2 * D_B]
        v = zs[:, 2 * D_B:3 * D_B]
        xw = zs[:, 3 * D_B:3 * D_B + LANES]
        xa = zs[:, 3 * D_B + LANES:3 * D_B + 2 * LANES]
        xg = zs[:, 3 * D_B + 2 * LANES:3 * D_B + 3 * LANES]
        w_log = -_softplus(-(w0 + dots.mm_weights(jnp.tanh(xw), wup_ref[...]))) - 0.5
        lw = -jnp.exp(w_log)
        a = _sigmoid(a0 + dots.mm_weights(xa, aup_ref[...]))
        g = dots.mm_weights(_sigmoid(xg), gup_ref[...])
        kk = k * k_k
        kk = kk * lax.rsqrt(jnp.maximum(_mm(_bf(kk * kk), ones_h), 1e-24))
        k2 = k * (1.0 + (a - 1.0) * k_a)
        bv = kk * a
        cum = _cumsum_rows(tril_b, lw)
        c_last = cum[lc - 1:lc, :]
        e_neg = jnp.exp(-cum)
        e_end = jnp.exp(c_last - cum)
        rt = _bf(r * jnp.exp(cum))
        at = _bf(-kk * jnp.exp(cum - lw))
        bt = _bf(bv * e_neg)
        kt = _bf(k2 * e_neg)
        bh = _bf(bv * e_end)
        kh = _bf(k2 * e_end)
        vb = _bf(v)
        prep.append(dict(rt=rt, at=at, bt=bt, kt=kt, bh=bh, kh=kh, vb=vb, p_last=jnp.exp(c_last),
                         tail=(gn_g, _mm(_bf(r * k2 * r_k), ones_h) * v, g)))
        yield
    chains = [(bi, h) for bi in range(bb) for h in range(H_B)]
    hs = lambda name: [prep[bi][name][:, h * HEAD_DIM:(h + 1) * HEAD_DIM] for bi, h in chains]
    s0 = [s_ref[bi, h] for bi, h in chains]
    at, rt, bt, kt, bh, kh, vb, p_last = (hs(n) for n in ('at', 'rt', 'bt', 'kt', 'bh', 'kh', 'vb', 'p_last'))
    n = range(len(chains))
    lhs = [jnp.concatenate([at[i], rt[i]], axis=0) for i in n]
    pair = [_nt(lhs[i], jnp.concatenate([bt[i], kt[i]], axis=0)) for i in n]
    a_mat = [jnp.where(strict2, pair[i][:lc], 0.0) for i in n]
    b_mat = [jnp.where(incl2, pair[i][lc:], 0.0) for i in n]
    ls = [_nt(lhs[i], _bf(s0[i])) for i in n]
    w = [ls[i][:lc] + _mm(_bf(a_mat[i]), jnp.concatenate([zeros_h, vb[i]], axis=0)) for i in n]
    yield
    pw = [a_mat[i][:, :lc] for i in n]
    u = [w[i] + _mm(_bf(pw[i]), _bf(w[i])) for i in n]
    for _ in range(n_dbl):
        yield
        pw = [_mm(_bf(pw[i]), _bf(pw[i])) for i in n]
        u = [u[i] + _mm(_bf(pw[i]), _bf(u[i])) for i in n]
    yield
    uv = [jnp.concatenate([_bf(u[i]), vb[i]], axis=0) for i in n]
    yh = [ls[i][lc:] + _mm(_bf(b_mat[i]), uv[i]) for i in n]
    s1 = [s0[i] * p_last[i] + _tn(uv[i], jnp.concatenate([bh[i], kh[i]], axis=0)) for i in n]
    for i, (bi, h) in enumerate(chains):
        s_ref[bi, h] = s1[i]
    yield
    for bi in range(bb):
        y = jnp.concatenate(yh[bi * H_B:(bi + 1) * H_B], axis=1)
        gn_g_, bonus, g = prep[bi]['tail']
        mu = _mm(_bf(y), ind)
        dev = y - mu
        var = _mm(_bf(dev * dev), ind)
        y_ref[bi, off:off + lc, D_A:D_A + D_B] = (
            (dev * lax.rsqrt(var + RWKV_GN_EPS) * gn_g_ + bonus) * g).astype(y_ref.dtype)


def _pad_b_cols(t):
    o = 3 * D_B
    pad = jnp.zeros(t.shape[:-1] + (LANES - R_W,), t.dtype)
    return jnp.concatenate([t[..., :o + R_W], pad, t[..., o + R_W:o + R_W + R_A], pad, t[..., o + R_W + R_A:]], axis=-1)


def _unpad_b_cols(t):
    o = 3 * D_B
    return jnp.concatenate([t[..., :o + R_W], t[..., o + LANES:o + LANES + R_A], t[..., o + 2 * LANES:]], axis=-1)


def _s5_abar(are_ref, aim_ref, ldt_ref):
    a_re = are_ref[...]
    a_im = aim_ref[...]
    dt = jnp.exp(ldt_ref[...])
    mag = jnp.exp(a_re * dt)
    return a_re, a_im, mag * jnp.cos(a_im * dt), mag * jnp.sin(a_im * dt)


def _s5_init(abar, hr0_ref, hi0_ref, bre_ref, bim_ref, hr_ref, hi_ref, bbr, bbi, dots):
    _bf = dots.cast
    a_re, a_im, ab_re, ab_im = abar
    hr_ref[...] = hr0_ref[...]
    hi_ref[...] = hi0_ref[...]
    inv = 1.0 / (a_re * a_re + a_im * a_im)
    q_re = ((ab_re - 1.0) * a_re + ab_im * a_im) * inv
    q_im = (ab_im * a_re - (ab_re - 1.0) * a_im) * inv
    bbr[...] = _bf(q_re * bre_ref[...] - q_im * bim_ref[...])
    bbi[...] = _bf(q_re * bim_ref[...] + q_im * bre_ref[...])


def _s5_compute(abar, z_ref, cre_ref, cim_ref, d_ref, wglu_ref, y_ref, hr_ref, hi_ref, bbr, bbi, *, off, bb, dots):
    _mm, _bf = dots.mm, dots.cast
    lc = CHUNK
    _, _, ab_re, ab_im = abar
    u = z_ref[:, off:off + lc, Z_C0:Z_C0 + D_C].reshape(bb * lc, D_C)
    ub = _bf(u)
    bu_re = _mm(ub, bbr[...])
    bu_im = _mm(ub, bbi[...])
    row8 = lax.broadcasted_iota(jnp.int32, (SUBLANES, 1), 0)
    pos = lax.broadcasted_iota(jnp.int32, (lc, 1), 0)
    xs = []
    for bi in range(bb):
        h_re = hr_ref[bi]
        h_im = hi_ref[bi]
        x_re = bu_re[bi * lc:(bi + 1) * lc]
        x_im = bu_im[bi * lc:(bi + 1) * lc]
        head_re = x_re[:SUBLANES] + jnp.where(row8 == 0, ab_re * h_re - ab_im * h_im, 0.0)
        head_im = x_im[:SUBLANES] + jnp.where(row8 == 0, ab_re * h_im + ab_im * h_re, 0.0)
        xs.append((jnp.concatenate([head_re, x_re[SUBLANES:]], axis=0),
                   jnp.concatenate([head_im, x_im[SUBLANES:]], axis=0)))
    p_re, p_im = ab_re, ab_im
    shift = 1
    while shift < lc:
        yield
        nxt = []
        for x_re, x_im in xs:
            if shift < SUBLANES:
                keep = pos >= shift
                s_re = jnp.where(keep, pltpu.roll(x_re, shift, axis=0), 0.0)
                s_im = jnp.where(keep, pltpu.roll(x_im, shift, axis=0), 0.0)
                nxt.append((x_re + p_re * s_re - p_im * s_im, x_im + p_re * s_im + p_im * s_re))
            else:
                s_re, s_im = x_re[:lc - shift], x_im[:lc - shift]
                nxt.append((jnp.concatenate([x_re[:shift], x_re[shift:] + p_re * s_re - p_im * s_im], axis=0),
                            jnp.concatenate([x_im[:shift], x_im[shift:] + p_re * s_im + p_im * s_re], axis=0)))
        xs = nxt
        p_re, p_im = p_re * p_re - p_im * p_im, 2.0 * p_re * p_im
        shift *= 2
    yield
    for bi in range(bb):
        hr_ref[bi] = xs[bi][0][lc - 1:lc, :]
        hi_ref[bi] = xs[bi][1][lc - 1:lc, :]
    x_re = jnp.concatenate([x[0] for x in xs], axis=0)
    x_im = jnp.concatenate([x[1] for x in xs], axis=0)
    y = _mm(_bf(x_re), cre_ref[...]) - _mm(_bf(x_im), cim_ref[...]) + d_ref[...] * u
    y = 0.5 * y * (1.0 + jnp.tanh(math.sqrt(2.0 / math.pi) * (y + 0.044715 * (y * y * y))))
    y = y * _sigmoid(dots.mm_weights(y, wglu_ref[...]))
    y_ref[:, off:off + lc, D_A + D_B:] = y.reshape(bb, lc, D_C).astype(y_ref.dtype)


def _block_diag(t):
    g, r, c = t.shape
    eye = jnp.eye(g, dtype=t.dtype)
    return (t[:, :, None, :] * eye[:, None, :, None]).reshape(g * r, g * c)


def _mixers_kernel(x_ref, xn_ref, win_ref, c0_ref, n0_ref, m0_ref, cv0_ref, cw_ref, bi_ref, bf_ref, ng_ref,
                   s0_ref, sh0_ref, mu_ref, vec_ref, wup_ref, aup_ref, gup_ref,
                   hr0_ref, hi0_ref, are_ref, aim_ref, ldt_ref, bre_ref, bim_ref, cre_ref, cim_ref,
                   d_ref, wglu_ref,
                   y_ref, c_ref, n_ref, m_ref, cv_ref, s_ref, sh_ref, hr_ref, hi_ref,
                   z2_ref, cbuf, bbr, bbi, *, ls, npad, bb, exact):
    step = pl.program_id(1)
    slot = step % 2
    dots = _Dots(exact)
    abar = _s5_abar(are_ref, aim_ref, ldt_ref)

    def project(src_ref, dst_slot, at_step):
        xb = []
        for bi in range(bb):
            x = src_ref[bi]
            if npad:
                pos = at_step * ls + lax.broadcasted_iota(jnp.int32, (ls, 1), 0)
                x = jnp.where(pos >= npad, x, 0.0)
            xb.append(_bf(x))
        for c0_, width in ((Z_C0, D_C), (Z_B0, ZB_W), (Z_GI0, 2 * LANES), (Z_A0, ZA_W)):
            for bi in range(bb):
                z2_ref[dst_slot, bi, :, c0_:c0_ + width] = _mm(xb[bi], win_ref[:, c0_:c0_ + width])
                yield

    @pl.when(step == 0)
    def _():
        _mlstm_init(c0_ref, n0_ref, m0_ref, cv0_ref, c_ref, n_ref, m_ref, cbuf)
        s_ref[...] = s0_ref[...]
        sh_ref[...] = sh0_ref[...]
        _s5_init(abar, hr0_ref, hi0_ref, bre_ref, bim_ref, hr_ref, hi_ref, bbr, bbi, dots)
        for _ in project(x_ref, 0, 0):
            pass

    z_ref = z2_ref.at[slot]

    def rwkv():
        for off in range(0, ls, CHUNK):
            yield from _rwkv_compute(z_ref, mu_ref, vec_ref, wup_ref, aup_ref, gup_ref, y_ref, s_ref, sh_ref,
                                     off=off, bb=bb, dots=dots)

    def s5():
        for off in range(0, ls, CHUNK):
            yield from _s5_compute(abar, z_ref, cre_ref, cim_ref, d_ref, wglu_ref, y_ref, hr_ref, hi_ref, bbr, bbi,
                                   off=off, bb=bb, dots=dots)

    gens = dict(R=rwkv(), S=s5(), P=project(xn_ref, 1 - slot, step + 1),
                M=_mlstm_compute(z_ref, cw_ref, bi_ref, bf_ref, ng_ref, y_ref, c_ref, n_ref, m_ref, cv_ref, cbuf,
                                 lc=ls, npad=npad, bb=bb, dots=dots))
    for name in MIXER_PHASE_ORDER * (ls // CHUNK):
        next(gens[name], None)
    pending = list(gens.values())
    while pending:
        for gen in list(pending):
            if next(gen, StopIteration) is StopIteration:
                pending.remove(gen)


def mixers(x3, st, p, npad, exact):
    b, lp, _ = x3.shape
    c0, n0, m0, cv0, s0, sh0, hr0, hi0 = st
    ls = min(STEP_ROWS, lp)
    bb = BATCH_PER_STEP
    assert b % bb == 0 and lp % ls == 0 and ls % CHUNK == 0
    gp = G_C * P_C
    n0p = jnp.zeros((b, SUBLANES, HEAD_DIM), F32).at[:, :H_A].set(n0)
    m0p = jnp.zeros((b, 1, LANES), F32).at[:, 0, :H_A].set(m0)
    cv0p = jnp.zeros((b, SUBLANES, 2 * D_A), F32).at[:, SUBLANES - (CONV_W - 1):].set(cv0)
    cwp = jnp.zeros((SUBLANES, 2 * D_A), F32).at[:CONV_W].set(p['mlstm_conv_w'])
    bi = jnp.zeros((1, LANES), F32).at[0, :H_A].set(p['mlstm_gate_b'][0])
    bf = jnp.zeros((1, LANES), F32).at[0, :H_A].set(p['mlstm_gate_b'][1])
    sh0p = jnp.zeros((b, SUBLANES, ZB_W), F32).at[:, SUBLANES - 1:].set(_pad_b_cols(sh0))
    mup = _pad_b_cols(p['rwkv_mu'].reshape(1, N_B_COLS))
    vecp = jnp.zeros((SUBLANES, D_B), F32).at[:6].set(p['rwkv_vecs'])
    wdt = _Dots(exact).dt
    wupp = jnp.zeros((LANES, D_B), wdt).at[:R_W].set(p['rwkv_w_up'].astype(wdt))
    aupp = jnp.zeros((LANES, D_B), wdt).at[:R_A].set(p['rwkv_a_up'].astype(wdt))
    flat = lambda t: t.reshape(1, gp).astype(F32)
    ldt = jnp.broadcast_to(p['s5_log_dt'].astype(F32)[:, None], (G_C, P_C)).reshape(1, gp)
    bre = _block_diag(jnp.swapaxes(p['s5_B_re'], 1, 2))
    bim = _block_diag(jnp.swapaxes(p['s5_B_im'], 1, 2))
    cre = _block_diag(jnp.swapaxes(p['s5_C_re'], 1, 2))
    cim = _block_diag(jnp.swapaxes(p['s5_C_im'], 1, 2))

    st_ = lambda *shape: pl.BlockSpec((bb,) + shape, lambda i, j: (i,) + (0,) * len(shape))
    par = lambda *shape: pl.BlockSpec(shape, lambda i, j: (0,) * len(shape))
    state_specs = (st_(H_A, HEAD_DIM, HEAD_DIM), st_(SUBLANES, HEAD_DIM), st_(1, LANES), st_(SUBLANES, 2 * D_A),
                   st_(H_B, HEAD_DIM, HEAD_DIM), st_(SUBLANES, ZB_W), st_(1, gp), st_(1, gp))
    state_shapes = ((b, H_A, HEAD_DIM, HEAD_DIM), (b, SUBLANES, HEAD_DIM), (b, 1, LANES), (b, SUBLANES, 2 * D_A),
                    (b, H_B, HEAD_DIM, HEAD_DIM), (b, SUBLANES, ZB_W), (b, 1, gp), (b, 1, gp))
    y, c1, n1, m1, cv1, s1, sh1, hr1, hi1 = pl.pallas_call(
        functools.partial(_mixers_kernel, ls=ls, npad=npad, bb=bb, exact=exact),
        out_shape=(jax.ShapeDtypeStruct((b, lp, D_MIX), BF16),) + tuple(
            jax.ShapeDtypeStruct(s, F32) for s in state_shapes),
        grid=(b // bb, lp // ls),
        in_specs=[pl.BlockSpec((bb, ls, D_MODEL), lambda i, j: (i, j, 0)),
                  pl.BlockSpec((bb, ls, D_MODEL), lambda i, j: (i, jnp.minimum(j + 1, lp // ls - 1), 0)),
                  par(D_MODEL, Z_N), *state_specs[:4],
                  par(SUBLANES, 2 * D_A), par(1, LANES), par(1, LANES), par(1, D_A),
                  *state_specs[4:6],
                  par(1, ZB_W), par(SUBLANES, D_B), par(LANES, D_B), par(LANES, D_B), par(R_G, D_B),
                  *state_specs[6:], par(1, gp), par(1, gp), par(1, gp),
                  par(D_C, gp), par(D_C, gp), par(gp, D_C), par(gp, D_C), par(1, D_C), par(D_C, D_C)],
        out_specs=(pl.BlockSpec((bb, ls, D_MIX), lambda i, j: (i, j, 0)),) + state_specs,
        scratch_shapes=[pltpu.VMEM((2, bb, ls, Z_N), F32), pltpu.VMEM((bb, ls + SUBLANES, 2 * D_A), F32),
                        pltpu.VMEM((D_C, gp), wdt), pltpu.VMEM((D_C, gp), wdt)],
        compiler_params=pltpu.CompilerParams(dimension_semantics=("arbitrary", "arbitrary"),
                                             vmem_limit_bytes=VMEM_LIMIT),
        name="mixers",
    )(x3, x3, p['w_in'], c0, n0p, m0p, cv0p, cwp, bi, bf, p['mlstm_norm_g'].reshape(1, D_A),
      s0, sh0p, mup, vecp, wupp, aupp, p['rwkv_g_up'].astype(wdt),
      hr0.reshape(b, 1, gp), hi0.reshape(b, 1, gp), flat(p['s5_A_re']), flat(p['s5_A_im']), ldt,
      bre, bim, cre.astype(wdt), cim.astype(wdt), p['s5_D'].reshape(1, D_C), p['s5_w_glu'].astype(wdt))
    states = (c1, n1[:, :H_A], m1[:, 0, :H_A], cv1[:, SUBLANES - (CONV_W - 1):],
              s1, _unpad_b_cols(sh1[:, SUBLANES - 1:]), hr1.reshape(b, G_C, P_C), hi1.reshape(b, G_C, P_C))
    return y, states


def _layer_norm(x, g, b):
    mu = jnp.mean(x, axis=-1, keepdims=True)
    dev = x - mu
    var = jnp.mean(dev * dev, axis=-1, keepdims=True)
    return dev * lax.rsqrt(var + LN_EPS) * g + b


def _argmax_lane(x, lane):
    m = jnp.max(x, axis=-1, keepdims=True)
    idx = jnp.min(jnp.where(x == m, lane, LANES), axis=-1, keepdims=True)
    return m, idx


def _out_proj_kernel(x_ref, y_ref, w_ref, g_ref, b_ref, wr_hi_ref, wr_lo_ref, br_ref, x1_ref, rt_ref, cnt_ref):
    @pl.when(pl.program_id(0) == 0)
    def _():
        cnt_ref[...] = jnp.zeros_like(cnt_ref)

    mix = _mm(y_ref[...], w_ref[...])
    x1 = _layer_norm(DEEP_ALPHA * x_ref[...] + mix, g_ref[...], b_ref[...])
    x1_ref[...] = x1
    tm = x1.shape[0]
    lane = lax.broadcasted_iota(jnp.int32, (tm, LANES), 1)
    x_hi = _bf(x1)
    x_lo = _bf(x1 - x_hi.astype(F32))
    fine = _mm(x_hi, wr_hi_ref[...])
    coarse = fine + _mm(x_hi, wr_lo_ref[...]) + _mm(x_lo, wr_hi_ref[...])
    logit = jnp.where(lane < N_GROUPS, coarse, fine) + br_ref[...]
    lco = jnp.where(lane < N_GROUPS, logit, NEG)
    mco, grp = _argmax_lane(lco, lane)
    p_grp = 1.0 / jnp.sum(jnp.exp(lco - mco), axis=-1, keepdims=True)
    lo_lane = N_GROUPS + grp * EXP_PER_GROUP
    lfi = jnp.where((lane >= lo_lane) & (lane < lo_lane + EXP_PER_GROUP), logit, NEG)
    m0, l0 = _argmax_lane(lfi, lane)
    m1, l1 = _argmax_lane(jnp.where(lane == l0, NEG, lfi), lane)
    t1 = jnp.exp(m1 - m0)
    g0 = p_grp / (1.0 + t1)
    g1 = p_grp * t1 / (1.0 + t1)
    onehot = jnp.where((lane == l0) | (lane == l1), 1.0, 0.0)
    before = _mm(_tril(tm, strict=True).astype(BF16), _bf(onehot)) + cnt_ref[...]
    rank0 = jnp.sum(jnp.where(lane == l0, before, 0.0), axis=-1, keepdims=True)
    rank1 = jnp.sum(jnp.where(lane == l1, before, 0.0), axis=-1, keepdims=True)
    cnt_ref[...] += jnp.sum(onehot, axis=0, keepdims=True)
    vals = (l0 - N_GROUPS).astype(F32), (l1 - N_GROUPS).astype(F32), g0, g1, rank0, rank1
    out = jnp.zeros((tm, LANES), F32)
    for j, val in enumerate(vals):
        out = jnp.where(lane == j, val, out)
    rt_ref[...] = out


def out_proj_router(x2d, ymix, w_out_bf16, ln_g, ln_b, w_coarse, b_coarse, w_fine, b_fine):
    t = x2d.shape[0]
    tm = min(TM, t)
    wr = jnp.concatenate([w_coarse, w_fine, jnp.zeros((D_MODEL, LANES - N_GROUPS - N_EXP), F32)], axis=1)
    wr_hi = wr.astype(BF16)
    wr_lo = (wr - wr_hi.astype(F32)).astype(BF16)
    br = jnp.concatenate([b_coarse, b_fine, jnp.zeros((LANES - N_GROUPS - N_EXP,), F32)]).reshape(1, LANES)
    tok = lambda w: pl.BlockSpec((tm, w), lambda i: (i, 0))
    par = lambda *shape: pl.BlockSpec(shape, lambda i: (0,) * len(shape))
    x1, route, cnt = pl.pallas_call(
        _out_proj_kernel,
        out_shape=(jax.ShapeDtypeStruct((t, D_MODEL), F32), jax.ShapeDtypeStruct((t, LANES), F32),
                   jax.ShapeDtypeStruct((1, LANES), F32)),
        grid=(t // tm,),
        in_specs=[tok(D_MODEL), tok(D_MIX), par(D_MIX, D_MODEL), par(1, D_MODEL), par(1, D_MODEL),
                  par(D_MODEL, LANES), par(D_MODEL, LANES), par(1, LANES)],
        out_specs=(tok(D_MODEL), tok(LANES), par(1, LANES)),
        compiler_params=pltpu.CompilerParams(dimension_semantics=("arbitrary",), vmem_limit_bytes=VMEM_LIMIT),
        name="out_proj_router",
    )(x2d, ymix, w_out_bf16, ln_g.reshape(1, D_MODEL), ln_b.reshape(1, D_MODEL), wr_hi, wr_lo, br)
    return x1, route, cnt[0, N_GROUPS:N_GROUPS + N_EXP]


def _row_copy(src_hbm, dst, sem, tok, r):
    return pltpu.make_async_copy(src_hbm.at[pl.ds(tok, 1)], dst.at[pl.ds(r, 1)], sem)


def _dispatch_kernel(d_ref, zb_ref, x1_ref, xs_out, zbuf, sem, zsem, *, tm):
    @pl.when(pl.program_id(0) == 0)
    def _():
        zbuf[...] = jnp.zeros_like(zbuf)
        fill = lambda k: pltpu.make_async_copy(zbuf, xs_out.at[pl.ds(zb_ref[k] * MOE_BLOCK, MOE_BLOCK)], zsem)
        for k in range(zb_ref.shape[0]):
            pl.when(zb_ref[k] >= 0)(lambda k=k: fill(k).start())
        for k in range(zb_ref.shape[0]):
            pl.when(zb_ref[k] >= 0)(lambda k=k: fill(k).wait())

    base = pl.program_id(0) * (tm * TOP_K)
    for r in range(tm):
        for j in range(TOP_K):
            pltpu.make_async_copy(x1_ref.at[pl.ds(r, 1)], xs_out.at[pl.ds(d_ref[base + r * TOP_K + j], 1)], sem).start()
    for j in range(TOP_K):
        pltpu.make_async_copy(x1_ref, xs_out.at[pl.ds(0, tm)], sem).wait()


def dispatch_rows(x1, dest, zero_blocks, n_rows):
    t = x1.shape[0]
    tm = min(TM, t)
    grid_spec = pltpu.PrefetchScalarGridSpec(
        num_scalar_prefetch=2,
        grid=(t // tm,),
        in_specs=[pl.BlockSpec((tm, D_MODEL), lambda i, d, zb: (i, 0))],
        out_specs=pl.BlockSpec(memory_space=pl.ANY),
        scratch_shapes=[pltpu.VMEM((MOE_BLOCK, D_MODEL), F32), pltpu.SemaphoreType.DMA, pltpu.SemaphoreType.DMA],
    )
    return pl.pallas_call(
        functools.partial(_dispatch_kernel, tm=tm),
        out_shape=jax.ShapeDtypeStruct((n_rows, D_MODEL), F32),
        grid_spec=grid_spec,
        compiler_params=pltpu.CompilerParams(dimension_semantics=("arbitrary",), vmem_limit_bytes=VMEM_LIMIT,
                                             has_side_effects=True),
        name="dispatch_rows",
    )(dest, zero_blocks, x1)


def _ffn_kernel(be_ref, xs_ref, wg_ref, wu_ref, wd_ref, ys_ref, wg_b, wu_b, wd_b):
    i = pl.program_id(0)

    @pl.when((i == 0) | (be_ref[i] != be_ref[jnp.maximum(i - 1, 0)]))
    def _():
        wg_b[...] = _bf(wg_ref[...])
        wu_b[...] = _bf(wu_ref[...])
        wd_b[...] = _bf(wd_ref[...])

    xb = xs_ref[...].astype(BF16)
    hg = _mm(xb, wg_b[...])
    hu = _mm(xb, wu_b[...])
    h = (hg * _sigmoid(hg) * hu).astype(BF16)
    ys_ref[...] = _mm(h, wd_b[...])


def expert_ffn(xs, block_e, wg, wu, wd, layer):
    n_rows = xs.shape[0]
    grid_spec = pltpu.PrefetchScalarGridSpec(
        num_scalar_prefetch=1,
        grid=(n_rows // MOE_BLOCK,),
        in_specs=[pl.BlockSpec((MOE_BLOCK, D_MODEL), lambda i, be: (i, 0)),
                  pl.BlockSpec((None, None, D_MODEL, D_EXPERT), lambda i, be: (layer, be[i], 0, 0)),
                  pl.BlockSpec((None, None, D_MODEL, D_EXPERT), lambda i, be: (layer, be[i], 0, 0)),
                  pl.BlockSpec((None, None, D_EXPERT, D_MODEL), lambda i, be: (layer, be[i], 0, 0))],
        out_specs=pl.BlockSpec((MOE_BLOCK, D_MODEL), lambda i, be: (i, 0)),
        scratch_shapes=[pltpu.VMEM((D_MODEL, D_EXPERT), BF16), pltpu.VMEM((D_MODEL, D_EXPERT), BF16),
                        pltpu.VMEM((D_EXPERT, D_MODEL), BF16)],
    )
    return pl.pallas_call(
        _ffn_kernel,
        out_shape=jax.ShapeDtypeStruct((n_rows, D_MODEL), F32),
        grid_spec=grid_spec,
        compiler_params=pltpu.CompilerParams(dimension_semantics=("arbitrary",), vmem_limit_bytes=VMEM_LIMIT),
        name="expert_ffn",
    )(block_e, xs, wg, wu, wd)


def _combine_kernel(d_ref, x1_ref, rt_ref, g_ref, b_ref, ys_hbm, o_ref, buf, sem, *, tm):
    i = pl.program_id(0)
    n = pl.num_programs(0)
    slot = i % GATHER_SLOTS

    def gather(tile, slot_):
        base = tile * (tm * TOP_K)
        for r in range(tm):
            for j in range(TOP_K):
                _row_copy(ys_hbm, buf.at[slot_, j], sem.at[slot_], d_ref[base + r * TOP_K + j], r).start()

    def wait_tile(slot_):
        for j in range(TOP_K):
            pltpu.make_async_copy(ys_hbm.at[pl.ds(0, tm)], buf.at[slot_, j], sem.at[slot_]).wait()

    @pl.when(i == 0)
    def _():
        for j in range(GATHER_SLOTS - 1):
            gather(jnp.minimum(j, n - 1), j)

    ahead = i + (GATHER_SLOTS - 1)
    gather(jnp.minimum(ahead, n - 1), ahead % GATHER_SLOTS)
    wait_tile(slot)

    rt = rt_ref[...]
    ffn = rt[:, 2:3] * buf[slot, 0] + rt[:, 3:4] * buf[slot, 1]
    o_ref[...] = _layer_norm(DEEP_ALPHA * x1_ref[...] + ffn, g_ref[...], b_ref[...])

    @pl.when(i == n - 1)
    def _():
        for j in range(1, GATHER_SLOTS):
            wait_tile((i + j) % GATHER_SLOTS)


def combine_ln(x1, route, dest, ys, ln_g, ln_b):
    t = x1.shape[0]
    tm = min(TM, t)
    grid_spec = pltpu.PrefetchScalarGridSpec(
        num_scalar_prefetch=1,
        grid=(t // tm,),
        in_specs=[pl.BlockSpec((tm, D_MODEL), lambda i, d: (i, 0)),
                  pl.BlockSpec((tm, LANES), lambda i, d: (i, 0)),
                  pl.BlockSpec((1, D_MODEL), lambda i, d: (0, 0)),
                  pl.BlockSpec((1, D_MODEL), lambda i, d: (0, 0)),
                  pl.BlockSpec(memory_space=pl.ANY)],
        out_specs=pl.BlockSpec((tm, D_MODEL), lambda i, d: (i, 0)),
        scratch_shapes=[pltpu.VMEM((GATHER_SLOTS, TOP_K, tm, D_MODEL), F32),
                        pltpu.SemaphoreType.DMA((GATHER_SLOTS,))],
    )
    return pl.pallas_call(
        functools.partial(_combine_kernel, tm=tm),
        out_shape=jax.ShapeDtypeStruct((t, D_MODEL), F32),
        grid_spec=grid_spec,
        compiler_params=pltpu.CompilerParams(dimension_semantics=("arbitrary",), vmem_limit_bytes=VMEM_LIMIT),
        name="combine_ln",
    )(dest, x1, route, ln_g.reshape(1, D_MODEL), ln_b.reshape(1, D_MODEL), ys)


def route_tables(route, counts):
    t = route.shape[0]
    n_blocks = -(-(t * TOP_K) // MOE_BLOCK) + N_EXP
    counts = counts.astype(jnp.int32)
    padded = (counts + MOE_BLOCK - 1) // MOE_BLOCK * MOE_BLOCK
    pad_end = jnp.cumsum(padded)
    pad_start = pad_end - padded
    flat_e = route[:, :TOP_K].astype(jnp.int32).reshape(-1)
    dest = pad_start[flat_e] + route[:, 4:4 + TOP_K].astype(jnp.int32).reshape(-1)
    first_row = jnp.arange(n_blocks, dtype=jnp.int32) * MOE_BLOCK
    block_e = jnp.minimum(jnp.sum((pad_end[None, :] <= first_row[:, None]).astype(jnp.int32), axis=1), N_EXP - 1)
    last_blk = jnp.where(padded > counts, pad_end // MOE_BLOCK - 1, -1)
    tail_blk = pad_end[-1] // MOE_BLOCK + jnp.arange(N_EXP, dtype=jnp.int32)
    zero_blocks = jnp.concatenate([last_blk, jnp.where(tail_blk < n_blocks, tail_blk, -1)]).astype(jnp.int32)
    return block_e, dest, zero_blocks, n_blocks * MOE_BLOCK


def _prep_w_in(w_in):
    wa = w_in[:, :ZA_W]
    gi = w_in[:, ZA_W:ZA_W + H_A]
    gf = w_in[:, ZA_W + H_A:N_A_COLS]
    wb = _pad_b_cols(w_in[:, N_A_COLS:N_A_COLS + N_B_COLS])
    wc = w_in[:, N_A_COLS + N_B_COLS:]
    gpad = jnp.zeros((D_MODEL, LANES - H_A), w_in.dtype)
    return jnp.concatenate([wa, wb, gi, gpad, gf, gpad, wc], axis=1).astype(BF16)


def _trunk_layer(x3, st, npad, p, experts, layer, exact):
    b, lp, _ = x3.shape
    x2d = x3.reshape(b * lp, D_MODEL)
    ymix, st_new = mixers(x3, st, p, npad, exact)
    x1, route, counts = out_proj_router(x2d, ymix.reshape(b * lp, D_MIX), p['w_out'],
                                        p['ln_g'][0], p['ln_b'][0], p['moe_w_coarse'], p['moe_b_coarse'],
                                        p['moe_w_fine'], p['moe_b_fine'])
    block_e, dest, zero_blocks, n_rows = route_tables(route, counts)
    xs = dispatch_rows(x1, dest, zero_blocks, n_rows)
    ys = expert_ffn(xs, block_e, *experts, layer)
    x2 = combine_ln(x1, route, dest, ys, p['ln_g'][1], p['ln_b'][1])
    return x2.reshape(b, lp, D_MODEL), st_new


def _run_trunk(x3, states, npad, params, experts, exact):
    new = [[] for _ in states]
    for l in range(DEPTH):
        p = {name: arr[l] for name, arr in params.items()}
        x3, st_new = _trunk_layer(x3, [s[l].astype(F32) for s in states], npad, p, experts, l, exact)
        for lst, s in zip(new, st_new):
            lst.append(s)
    return x3, [jnp.stack(lst) for lst in new]


def kernel(x_prompt, x_sample, state_mlstm_C, state_mlstm_n, state_mlstm_m, state_mlstm_conv, state_rwkv_S, state_rwkv_shift, state_s5_re, state_s5_im, meta_tokens, w_in, w_out, mlstm_conv_w, mlstm_gate_b, mlstm_norm_g, rwkv_mu, rwkv_vecs, rwkv_w_up, rwkv_a_up, rwkv_g_up, s5_A_re, s5_A_im, s5_log_dt, s5_B_re, s5_B_im, s5_C_re, s5_C_im, s5_D, s5_w_glu, ln_g, ln_b, moe_w_coarse, moe_b_coarse, moe_w_fine, moe_b_fine, moe_w_gate, moe_w_up, moe_w_down):
    params = dict(w_in=jax.vmap(_prep_w_in)(w_in), w_out=w_out.astype(BF16), mlstm_conv_w=mlstm_conv_w,
                  mlstm_gate_b=mlstm_gate_b, mlstm_norm_g=mlstm_norm_g, rwkv_mu=rwkv_mu, rwkv_vecs=rwkv_vecs,
                  rwkv_w_up=rwkv_w_up, rwkv_a_up=rwkv_a_up, rwkv_g_up=rwkv_g_up, s5_A_re=s5_A_re, s5_A_im=s5_A_im,
                  s5_log_dt=s5_log_dt, s5_B_re=s5_B_re, s5_B_im=s5_B_im, s5_C_re=s5_C_re, s5_C_im=s5_C_im,
                  s5_D=s5_D, s5_w_glu=s5_w_glu, ln_g=ln_g, ln_b=ln_b, moe_w_coarse=moe_w_coarse,
                  moe_b_coarse=moe_b_coarse, moe_w_fine=moe_w_fine, moe_b_fine=moe_b_fine,
                  )
    experts = (moe_w_gate, moe_w_up, moe_w_down)
    state_in = [state_mlstm_C, state_mlstm_n, state_mlstm_m, state_mlstm_conv,
                state_rwkv_S, state_rwkv_shift, state_s5_re, state_s5_im]

    b, seq, _ = x_prompt.shape
    real = N_META + seq
    npad = (-real) % SEQ_PAD_MULT
    meta = jnp.broadcast_to(meta_tokens.astype(x_prompt.dtype)[None], (b, N_META, D_MODEL))
    xp = jnp.concatenate([jnp.zeros((b, npad, D_MODEL), x_prompt.dtype), meta, x_prompt], axis=1)
    fresh = [jnp.zeros((DEPTH, b) + s.shape[2:], F32) for s in state_in]
    yp, pst = _run_trunk(xp, fresh, npad, params, experts, exact=False)
    y_prompt = yp[:, npad + N_META:]
    pst = [s.astype(r.dtype) for s, r in zip(pst, state_in)]

    y_sample, sst = _run_trunk(x_sample, state_in, 0, params, experts, exact=True)
    sst = [s.astype(r.dtype) for s, r in zip(sst, state_in)]
    return (y_prompt, y_sample, *pst, *sst)
```

```python
import functools
import math

import jax
import jax.numpy as jnp
from jax import lax
from jax.experimental import pallas as pl
from jax.experimental.pallas import tpu as pltpu

F32 = jnp.float32
BF16 = jnp.bfloat16
HI = lax.Precision.HIGHEST

D_MODEL = 1024
DEPTH = 2
N_META = 16
HEAD_DIM = 64
D_A = 384
H_A = D_A // HEAD_DIM
CONV_W = 4
D_B = 384
H_B = D_B // HEAD_DIM
R_W = 64
R_A = 64
R_G = 128
RWKV_GN_EPS = 64e-5
D_C = 256
C_GROUP = 16
G_C = D_C // C_GROUP
P_C = 64
D_MIX = D_A + D_B + D_C
N_A_COLS = 4 * D_A + 2 * H_A
N_B_COLS = 3 * D_B + R_W + R_A + R_G
N_GROUPS = 4
EXP_PER_GROUP = 8
N_EXP = N_GROUPS * EXP_PER_GROUP
TOP_K = 2
D_EXPERT = 512
MOE_BLOCK = 256
DEEP_ALPHA = (2 * DEPTH) ** 0.25
LN_EPS = 1e-5

LANES = 128
SUBLANES = 8
VMEM_LIMIT = 48 * 1024 * 1024

ZA_W = 4 * D_A
ZB_W = 3 * D_B + 3 * LANES
Z_A0 = 0
Z_B0 = ZA_W
Z_GI0 = Z_B0 + ZB_W
Z_GF0 = Z_GI0 + LANES
Z_C0 = Z_GF0 + LANES
Z_N = Z_C0 + D_C
SEQ_PAD_MULT = 256
CHUNK = 64
STEP_ROWS = 128
GATHER_SLOTS = 3
MIXER_PHASE_ORDER = "RPRPS" "RMP" "RSP" "RS" "RS" "RS" "RS" "RS" "RRSM"
BATCH_PER_STEP = 2
TM = 256
NEG = -1e30


def _nt(a, b, precision=None):
    return lax.dot_general(a, b, (((1,), (1,)), ((), ())), precision=precision, preferred_element_type=F32)


def _tn(a, b, precision=None):
    return lax.dot_general(a, b, (((0,), (0,)), ((), ())), precision=precision, preferred_element_type=F32)


def _mm(a, b, precision=None):
    return jnp.dot(a, b, precision=precision, preferred_element_type=F32)


class _Dots:
    def __init__(self, exact):
        self.dt = F32 if exact else BF16
        self.prec = HI if exact else None

    def cast(self, x):
        return x.astype(self.dt)

    def mm(self, a, b):
        return _mm(a, b, self.prec)

    def nt(self, a, b):
        return _nt(a, b, self.prec)

    def tn(self, a, b):
        return _tn(a, b, self.prec)

    @staticmethod
    def mm_weights(a, w):
        return _mm(a.astype(BF16), w.astype(BF16))


def _bf(x):
    return x.astype(BF16)


def _cumsum_rows(tril_bf16, x):
    hi = _bf(x)
    lo = _bf(x - hi.astype(F32))
    return _mm(tril_bf16, hi) + _mm(tril_bf16, lo)


def _sigmoid(x):
    return 1.0 / (1.0 + jnp.exp(-x))


def _softplus(x):
    return jnp.maximum(x, 0.0) + jnp.log1p(jnp.exp(-jnp.abs(x)))


def _head_indicator(n, scale):
    shift = jnp.int32(int(math.log2(HEAD_DIM)))
    r = lax.shift_right_logical(lax.broadcasted_iota(jnp.int32, (n, n), 0), shift)
    c = lax.shift_right_logical(lax.broadcasted_iota(jnp.int32, (n, n), 1), shift)
    return jnp.where(r == c, scale, 0.0).astype(F32)


def _tril(n, strict=False):
    r = lax.broadcasted_iota(jnp.int32, (n, n), 0)
    c = lax.broadcasted_iota(jnp.int32, (n, n), 1)
    return (r > c) if strict else (r >= c)


def _mlstm_init(c0_ref, n0_ref, m0_ref, cv0_ref, c_ref, n_ref, m_ref, cbuf):
    c_ref[...] = c0_ref[...]
    n_ref[...] = n0_ref[...]
    m_ref[...] = m0_ref[...]
    cbuf[:, 0:SUBLANES, :] = cv0_ref[...]


def _mlstm_compute(z_ref, cw_ref, bi_ref, bf_ref, ng_ref,
                   y_ref, c_ref, n_ref, m_ref, cv_ref, cbuf, *, lc, npad, bb, dots):
    _mm, _nt, _tn, _bf, cdt = dots.mm, dots.nt, dots.tn, dots.cast, dots.dt
    c = pl.program_id(1)
    causal = _tril(lc)
    tril_b = causal.astype(BF16)
    eye8 = (lax.broadcasted_iota(jnp.int32, (SUBLANES, LANES), 0)
            == lax.broadcasted_iota(jnp.int32, (SUBLANES, LANES), 1)).astype(cdt)
    lane = lax.broadcasted_iota(jnp.int32, (1, LANES), 1)
    sub = lax.broadcasted_iota(jnp.int32, (SUBLANES, 1), 0)
    ind = _head_indicator(D_A, 1.0 / HEAD_DIM).astype(cdt)
    prep = []
    for bi in range(bb):
        zq = z_ref[bi, :, Z_A0:Z_A0 + ZA_W]
        cbuf[bi, SUBLANES:, :] = zq[:, :2 * D_A]
        acc = zq[:, :2 * D_A] * cw_ref[CONV_W - 1:CONV_W, :]
        for j in range(CONV_W - 1):
            d = CONV_W - 1 - j
            acc = acc + cbuf[bi, SUBLANES - d:SUBLANES - d + lc, :] * cw_ref[j:j + 1, :]
        qk = acc * _sigmoid(acc)
        last_rows = cbuf[bi, lc:lc + SUBLANES, :]
        cbuf[bi, 0:SUBLANES, :] = last_rows
        cv_ref[bi] = last_rows

        logi = z_ref[bi, :, Z_GI0:Z_GI0 + LANES] + bi_ref[...]
        logf = -_softplus(-(z_ref[bi, :, Z_GF0:Z_GF0 + LANES] + bf_ref[...]))
        if npad:
            pos = c * lc + lax.broadcasted_iota(jnp.int32, (lc, 1), 0)
            logi = jnp.where(pos >= npad, logi, NEG)
            logf = jnp.where(pos >= npad, logf, 0.0)
        bcum = _cumsum_rows(tril_b, logf)
        gcol = logi - bcum
        g_hi = _bf(gcol)
        g_lo = _bf(gcol - g_hi.astype(F32))
        grow_all = _nt(eye8, g_hi) + _nt(eye8, g_lo)
        prep.append(dict(bcum=bcum, gcol=gcol, grow=grow_all, m_row=m_ref[bi], n_all=n_ref[bi],
                         q=_bf(qk[:, :D_A]), k=_bf(qk[:, D_A:] * (HEAD_DIM ** -0.5)), v=zq[:, 2 * D_A:3 * D_A],
                         o=_sigmoid(zq[:, 3 * D_A:4 * D_A])))
        yield
    chains = [(bi, h) for bi in range(bb) for h in range(H_A)]
    n = range(len(chains))
    hs = lambda name: [prep[bi][name][:, h * HEAD_DIM:(h + 1) * HEAD_DIM] for bi, h in chains]
    col = lambda name: [prep[bi][name][:, h:h + 1] for bi, h in chains]
    c0 = [c_ref[bi, h] for bi, h in chains]
    n0 = [prep[bi]['n_all'][h:h + 1, :] for bi, h in chains]
    q, k, v = hs('q'), hs('k'), hs('v')
    bcol, gcl, m0 = col('bcum'), col('gcol'), col('m_row')
    dm = [jnp.where(causal, bcol[i] + prep[bi]['grow'][h:h + 1, :], NEG) for i, (bi, h) in enumerate(chains)]
    g = [bcol[i] + m0[i] for i in n]
    m = [jnp.maximum(g[i], jnp.max(dm[i], axis=-1, keepdims=True)) for i in n]
    w_inter = [jnp.exp(g[i] - m[i]) for i in n]
    s = [_nt(q[i], k[i]) * jnp.exp(dm[i] - m[i]) for i in n]
    yield
    num =[w_inter[i] * _nt(q[i], _bf(c0[i])) + _mm(_bf(s[i]), _bf(v[i])) for i in n]
    den = [w_inter[i] * jnp.sum(q[i].astype(F32) * n0[i], axis=-1, keepdims=True)
           + jnp.sum(s[i], axis=-1, keepdims=True) for i in n]
    hh = [num[i] / jnp.maximum(jnp.abs(den[i]), jnp.exp(-m[i])) for i in n]
    yield
    m_end =[m[i][lc - 1:lc, :] for i in n]
    w_end = [jnp.exp(bcol[i][lc - 1:lc, :] + gcl[i] - m_end[i]) for i in n]
    dec = [jnp.exp(g[i][lc - 1:lc, :] - m_end[i]) for i in n]
    c1 = [dec[i] * c0[i] + _tn(_bf(v[i] * w_end[i]), k[i]) for i in n]
    n1 = [dec[i] * n0[i] + jnp.sum(k[i].astype(F32) * w_end[i], axis=0, keepdims=True) for i in n]
    for i, (bi, h) in enumerate(chains):
        c_ref[bi, h] = c1[i]
    yield
    for bi in range(bb):
        m_new = prep[bi]['m_row']
        n_new = prep[bi]['n_all']
        for h in range(H_A):
            m_new = jnp.where(lane == h, m_end[bi * H_A + h], m_new)
            n_new = jnp.where(sub == h, n1[bi * H_A + h], n_new)
        m_ref[bi] = m_new
        n_ref[bi] = n_new
        hcat = jnp.concatenate(hh[bi * H_A:(bi + 1) * H_A], axis=1)
        mu = _mm(_bf(hcat), ind)
        dev = hcat - mu
        var = _mm(_bf(dev * dev), ind)
        y_ref[bi, :, 0:D_A] = (prep[bi]['o'] * (dev * lax.rsqrt(var + LN_EPS) * ng_ref[...])).astype(y_ref.dtype)


def _rwkv_compute(z_ref, mu_ref, vec_ref, wup_ref, aup_ref, gup_ref, y_ref, s_ref, sh_ref, *, off, bb, dots):
    _mm, _nt, _tn, _bf, cdt = dots.mm, dots.nt, dots.tn, dots.cast, dots.dt
    lc = CHUNK
    row = lax.broadcasted_iota(jnp.int32, (lc, 1), 0)
    w0, a0, k_k, k_a, r_k, gn_g = (vec_ref[i:i + 1, :] for i in range(6))
    ones_h = _head_indicator(D_B, 1.0).astype(cdt)
    ind = _head_indicator(D_B, 1.0 / HEAD_DIM).astype(cdt)
    incl = _tril(lc)
    tril_b = incl.astype(BF16)
    r2 = lax.broadcasted_iota(jnp.int32, (lc, 2 * lc), 0)
    c2 = lax.broadcasted_iota(jnp.int32, (lc, 2 * lc), 1)
    c2 = jnp.where(c2 >= lc, c2 - lc, c2)
    strict2 = r2 > c2
    incl2 = r2 >= c2
    zeros_h = jnp.zeros((lc, HEAD_DIM), cdt)
    n_dbl = int(math.log2(lc)) - 1
    prep = []
    for bi in range(bb):
        z = z_ref[bi, off:off + lc, Z_B0:Z_B0 + ZB_W]
        zprev = jnp.where(row == 0, sh_ref[bi, SUBLANES - 1:SUBLANES, :], pltpu.roll(z, 1, axis=0))
        zs = z + mu_ref[...] * (zprev - z)
        sh_ref[bi] = z[lc - SUBLANES:lc, :]
        r = zs[:, 0:D_B]
        k = zs[:, D_B:2 * D_B]
        v = zs[:, 2 * D_B:3 * D_B]
        xw = zs[:, 3 * D_B:3 * D_B + LANES]
        xa = zs[:, 3 * D_B + LANES:3 * D_B + 2 * LANES]
        xg = zs[:, 3 * D_B + 2 * LANES:3 * D_B + 3 * LANES]
        w_log = -_softplus(-(w0 + dots.mm_weights(jnp.tanh(xw), wup_ref[...]))) - 0.5
        lw = -jnp.exp(w_log)
        a = _sigmoid(a0 + dots.mm_weights(xa, aup_ref[...]))
        g = dots.mm_weights(_sigmoid(xg), gup_ref[...])
        kk = k * k_k
        kk = kk * lax.rsqrt(jnp.maximum(_mm(_bf(kk * kk), ones_h), 1e-24))
        k2 = k * (1.0 + (a - 1.0) * k_a)
        bv = kk * a
        cum = _cumsum_rows(tril_b, lw)
        c_last = cum[lc - 1:lc, :]
        e_neg = jnp.exp(-cum)
        e_end = jnp.exp(c_last - cum)
        rt = _bf(r * jnp.exp(cum))
        at = _bf(-kk * jnp.exp(cum - lw))
        bt = _bf(bv * e_neg)
        kt = _bf(k2 * e_neg)
        bh = _bf(bv * e_end)
        kh = _bf(k2 * e_end)
        vb = _bf(v)
        prep.append(dict(rt=rt, at=at, bt=bt, kt=kt, bh=bh, kh=kh, vb=vb, p_last=jnp.exp(c_last),
                         tail=(gn_g, _mm(_bf(r * k2 * r_k), ones_h) * v, g)))
        yield
    chains = [(bi, h) for bi in range(bb) for h in range(H_B)]
    hs = lambda name: [prep[bi][name][:, h * HEAD_DIM:(h + 1) * HEAD_DIM] for bi, h in chains]
    s0 = [s_ref[bi, h] for bi, h in chains]
    at, rt, bt, kt, bh, kh, vb, p_last = (hs(n) for n in ('at', 'rt', 'bt', 'kt', 'bh', 'kh', 'vb', 'p_last'))
    n = range(len(chains))
    lhs = [jnp.concatenate([at[i], rt[i]], axis=0) for i in n]
    pair = [_nt(lhs[i], jnp.concatenate([bt[i], kt[i]], axis=0)) for i in n]
    a_mat = [jnp.where(strict2, pair[i][:lc], 0.0) for i in n]
    b_mat = [jnp.where(incl2, pair[i][lc:], 0.0) for i in n]
    ls = [_nt(lhs[i], _bf(s0[i])) for i in n]
    w = [ls[i][:lc] + _mm(_bf(a_mat[i]), jnp.concatenate([zeros_h, vb[i]], axis=0)) for i in n]
    yield
    pw = [a_mat[i][:, :lc] for i in n]
    u = [w[i] + _mm(_bf(pw[i]), _bf(w[i])) for i in n]
    for _ in range(n_dbl):
        yield
        pw = [_mm(_bf(pw[i]), _bf(pw[i])) for i in n]
        u = [u[i] + _mm(_bf(pw[i]), _bf(u[i])) for i in n]
    yield
    uv = [jnp.concatenate([_bf(u[i]), vb[i]], axis=0) for i in n]
    yh = [ls[i][lc:] + _mm(_bf(b_mat[i]), uv[i]) for i in n]
    s1 = [s0[i] * p_last[i] + _tn(uv[i], jnp.concatenate([bh[i], kh[i]], axis=0)) for i in n]
    for i, (bi, h) in enumerate(chains):
        s_ref[bi, h] = s1[i]
    yield
    for bi in range(bb):
        y = jnp.concatenate(yh[bi * H_B:(bi + 1) * H_B], axis=1)
        gn_g_, bonus, g = prep[bi]['tail']
        mu = _mm(_bf(y), ind)
        dev = y - mu
        var = _mm(_bf(dev * dev), ind)
        y_ref[bi, off:off + lc, D_A:D_A + D_B] = (
            (dev * lax.rsqrt(var + RWKV_GN_EPS) * gn_g_ + bonus) * g).astype(y_ref.dtype)


def _pad_b_cols(t):
    o = 3 * D_B
    pad = jnp.zeros(t.shape[:-1] + (LANES - R_W,), t.dtype)
    return jnp.concatenate([t[..., :o + R_W], pad, t[..., o + R_W:o + R_W + R_A], pad, t[..., o + R_W + R_A:]], axis=-1)


def _unpad_b_cols(t):
    o = 3 * D_B
    return jnp.concatenate([t[..., :o + R_W], t[..., o + LANES:o + LANES + R_A], t[..., o + 2 * LANES:]], axis=-1)


def _s5_abar(are_ref, aim_ref, ldt_ref):
    a_re = are_ref[...]
    a_im = aim_ref[...]
    dt = jnp.exp(ldt_ref[...])
    mag = jnp.exp(a_re * dt)
    return a_re, a_im, mag * jnp.cos(a_im * dt), mag * jnp.sin(a_im * dt)


def _s5_init(abar, hr0_ref, hi0_ref, bre_ref, bim_ref, hr_ref, hi_ref, bbr, bbi, dots):
    _bf = dots.cast
    a_re, a_im, ab_re, ab_im = abar
    hr_ref[...] = hr0_ref[...]
    hi_ref[...] = hi0_ref[...]
    inv = 1.0 / (a_re * a_re + a_im * a_im)
    q_re = ((ab_re - 1.0) * a_re + ab_im * a_im) * inv
    q_im = (ab_im * a_re - (ab_re - 1.0) * a_im) * inv
    bbr[...] = _bf(q_re * bre_ref[...] - q_im * bim_ref[...])
    bbi[...] = _bf(q_re * bim_ref[...] + q_im * bre_ref[...])


def _s5_compute(abar, z_ref, cre_ref, cim_ref, d_ref, wglu_ref, y_ref, hr_ref, hi_ref, bbr, bbi, *, off, bb, dots):
    _mm, _bf = dots.mm, dots.cast
    lc = CHUNK
    _, _, ab_re, ab_im = abar
    u = z_ref[:, off:off + lc, Z_C0:Z_C0 + D_C].reshape(bb * lc, D_C)
    ub = _bf(u)
    bu_re = _mm(ub, bbr[...])
    bu_im = _mm(ub, bbi[...])
    row8 = lax.broadcasted_iota(jnp.int32, (SUBLANES, 1), 0)
    pos = lax.broadcasted_iota(jnp.int32, (lc, 1), 0)
    xs = []
    for bi in range(bb):
        h_re = hr_ref[bi]
        h_im = hi_ref[bi]
        x_re = bu_re[bi * lc:(bi + 1) * lc]
        x_im = bu_im[bi * lc:(bi + 1) * lc]
        head_re = x_re[:SUBLANES] + jnp.where(row8 == 0, ab_re * h_re - ab_im * h_im, 0.0)
        head_im = x_im[:SUBLANES] + jnp.where(row8 == 0, ab_re * h_im + ab_im * h_re, 0.0)
        xs.append((jnp.concatenate([head_re, x_re[SUBLANES:]], axis=0),
                   jnp.concatenate([head_im, x_im[SUBLANES:]], axis=0)))
    p_re, p_im = ab_re, ab_im
    shift = 1
    while shift < lc:
        yield
        nxt = []
        for x_re, x_im in xs:
            if shift < SUBLANES:
                keep = pos >= shift
                s_re = jnp.where(keep, pltpu.roll(x_re, shift, axis=0), 0.0)
                s_im = jnp.where(keep, pltpu.roll(x_im, shift, axis=0), 0.0)
                nxt.append((x_re + p_re * s_re - p_im * s_im, x_im + p_re * s_im + p_im * s_re))
            else:
                s_re, s_im = x_re[:lc - shift], x_im[:lc - shift]
                nxt.append((jnp.concatenate([x_re[:shift], x_re[shift:] + p_re * s_re - p_im * s_im], axis=0),
                            jnp.concatenate([x_im[:shift], x_im[shift:] + p_re * s_im + p_im * s_re], axis=0)))
        xs = nxt
        p_re, p_im = p_re * p_re - p_im * p_im, 2.0 * p_re * p_im
        shift *= 2
    yield
    for bi in range(bb):
        hr_ref[bi] = xs[bi][0][lc - 1:lc, :]
        hi_ref[bi] = xs[bi][1][lc - 1:lc, :]
    x_re = jnp.concatenate([x[0] for x in xs], axis=0)
    x_im = jnp.concatenate([x[1] for x in xs], axis=0)
    y = _mm(_bf(x_re), cre_ref[...]) - _mm(_bf(x_im), cim_ref[...]) + d_ref[...] * u
    y = 0.5 * y * (1.0 + jnp.tanh(math.sqrt(2.0 / math.pi) * (y + 0.044715 * (y * y * y))))
    y = y * _sigmoid(dots.mm_weights(y, wglu_ref[...]))
    y_ref[:, off:off + lc, D_A + D_B:] = y.reshape(bb, lc, D_C).astype(y_ref.dtype)


def _block_diag(t):
    g, r, c = t.shape
    eye = jnp.eye(g, dtype=t.dtype)
    return (t[:, :, None, :] * eye[:, None, :, None]).reshape(g * r, g * c)


def _mixers_kernel(x_ref, xn_ref, win_ref, c0_ref, n0_ref, m0_ref, cv0_ref, cw_ref, bi_ref, bf_ref, ng_ref,
                   s0_ref, sh0_ref, mu_ref, vec_ref, wup_ref, aup_ref, gup_ref,
                   hr0_ref, hi0_ref, are_ref, aim_ref, ldt_ref, bre_ref, bim_ref, cre_ref, cim_ref,
                   d_ref, wglu_ref,
                   y_ref, c_ref, n_ref, m_ref, cv_ref, s_ref, sh_ref, hr_ref, hi_ref,
                   z2_ref, cbuf, bbr, bbi, *, ls, npad, bb, exact):
    step = pl.program_id(1)
    slot = step % 2
    dots = _Dots(exact)
    abar = _s5_abar(are_ref, aim_ref, ldt_ref)

    def project(src_ref, dst_slot, at_step):
        xb = []
        for bi in range(bb):
            x = src_ref[bi]
            if npad:
                pos = at_step * ls + lax.broadcasted_iota(jnp.int32, (ls, 1), 0)
                x = jnp.where(pos >= npad, x, 0.0)
            xb.append(_bf(x))
        for c0_, width in ((Z_C0, D_C), (Z_B0, ZB_W), (Z_GI0, 2 * LANES), (Z_A0, ZA_W)):
            for bi in range(bb):
                z2_ref[dst_slot, bi, :, c0_:c0_ + width] = _mm(xb[bi], win_ref[:, c0_:c0_ + width])
                yield

    @pl.when(step == 0)
    def _():
        _mlstm_init(c0_ref, n0_ref, m0_ref, cv0_ref, c_ref, n_ref, m_ref, cbuf)
        s_ref[...] = s0_ref[...]
        sh_ref[...] = sh0_ref[...]
        _s5_init(abar, hr0_ref, hi0_ref, bre_ref, bim_ref, hr_ref, hi_ref, bbr, bbi, dots)
        for _ in project(x_ref, 0, 0):
            pass

    z_ref = z2_ref.at[slot]

    def rwkv():
        for off in range(0, ls, CHUNK):
            yield from _rwkv_compute(z_ref, mu_ref, vec_ref, wup_ref, aup_ref, gup_ref, y_ref, s_ref, sh_ref,
                                     off=off, bb=bb, dots=dots)

    def s5():
        for off in range(0, ls, CHUNK):
            yield from _s5_compute(abar, z_ref, cre_ref, cim_ref, d_ref, wglu_ref, y_ref, hr_ref, hi_ref, bbr, bbi,
                                   off=off, bb=bb, dots=dots)

    gens = dict(R=rwkv(), S=s5(), P=project(xn_ref, 1 - slot, step + 1),
                M=_mlstm_compute(z_ref, cw_ref, bi_ref, bf_ref, ng_ref, y_ref, c_ref, n_ref, m_ref, cv_ref, cbuf,
                                 lc=ls, npad=npad, bb=bb, dots=dots))
    for name in MIXER_PHASE_ORDER * (ls // CHUNK):
        next(gens[name], None)
    pending = list(gens.values())
    while pending:
        for gen in list(pending):
            if next(gen, StopIteration) is StopIteration:
                pending.remove(gen)


def mixers(x3, st, p, npad, exact):
    b, lp, _ = x3.shape
    c0, n0, m0, cv0, s0, sh0, hr0, hi0 = st
    ls = min(STEP_ROWS, lp)
    bb = BATCH_PER_STEP
    assert b % bb == 0 and lp % ls == 0 and ls % CHUNK == 0
    gp = G_C * P_C
    n0p = jnp.zeros((b, SUBLANES, HEAD_DIM), F32).at[:, :H_A].set(n0)
    m0p = jnp.zeros((b, 1, LANES), F32).at[:, 0, :H_A].set(m0)
    cv0p = jnp.zeros((b, SUBLANES, 2 * D_A), F32).at[:, SUBLANES - (CONV_W - 1):].set(cv0)
    cwp = jnp.zeros((SUBLANES, 2 * D_A), F32).at[:CONV_W].set(p['mlstm_conv_w'])
    bi = jnp.zeros((1, LANES), F32).at[0, :H_A].set(p['mlstm_gate_b'][0])
    bf = jnp.zeros((1, LANES), F32).at[0, :H_A].set(p['mlstm_gate_b'][1])
    sh0p = jnp.zeros((b, SUBLANES, ZB_W), F32).at[:, SUBLANES - 1:].set(_pad_b_cols(sh0))
    mup = _pad_b_cols(p['rwkv_mu'].reshape(1, N_B_COLS))
    vecp = jnp.zeros((SUBLANES, D_B), F32).at[:6].set(p['rwkv_vecs'])
    wdt = _Dots(exact).dt
    wupp = jnp.zeros((LANES, D_B), wdt).at[:R_W].set(p['rwkv_w_up'].astype(wdt))
    aupp = jnp.zeros((LANES, D_B), wdt).at[:R_A].set(p['rwkv_a_up'].astype(wdt))
    flat = lambda t: t.reshape(1, gp).astype(F32)
    ldt = jnp.broadcast_to(p['s5_log_dt'].astype(F32)[:, None], (G_C, P_C)).reshape(1, gp)
    bre = _block_diag(jnp.swapaxes(p['s5_B_re'], 1, 2))
    bim = _block_diag(jnp.swapaxes(p['s5_B_im'], 1, 2))
    cre = _block_diag(jnp.swapaxes(p['s5_C_re'], 1, 2))
    cim = _block_diag(jnp.swapaxes(p['s5_C_im'], 1, 2))

    st_ = lambda *shape: pl.BlockSpec((bb,) + shape, lambda i, j: (i,) + (0,) * len(shape))
    par = lambda *shape: pl.BlockSpec(shape, lambda i, j: (0,) * len(shape))
    state_specs = (st_(H_A, HEAD_DIM, HEAD_DIM), st_(SUBLANES, HEAD_DIM), st_(1, LANES), st_(SUBLANES, 2 * D_A),
                   st_(H_B, HEAD_DIM, HEAD_DIM), st_(SUBLANES, ZB_W), st_(1, gp), st_(1, gp))
    state_shapes = ((b, H_A, HEAD_DIM, HEAD_DIM), (b, SUBLANES, HEAD_DIM), (b, 1, LANES), (b, SUBLANES, 2 * D_A),
                    (b, H_B, HEAD_DIM, HEAD_DIM), (b, SUBLANES, ZB_W), (b, 1, gp), (b, 1, gp))
    y, c1, n1, m1, cv1, s1, sh1, hr1, hi1 = pl.pallas_call(
        functools.partial(_mixers_kernel, ls=ls, npad=npad, bb=bb, exact=exact),
        out_shape=(jax.ShapeDtypeStruct((b, lp, D_MIX), BF16),) + tuple(
            jax.ShapeDtypeStruct(s, F32) for s in state_shapes),
        grid=(b // bb, lp // ls),
        in_specs=[pl.BlockSpec((bb, ls, D_MODEL), lambda i, j: (i, j, 0)),
                  pl.BlockSpec((bb, ls, D_MODEL), lambda i, j: (i, jnp.minimum(j + 1, lp // ls - 1), 0)),
                  par(D_MODEL, Z_N), *state_specs[:4],
                  par(SUBLANES, 2 * D_A), par(1, LANES), par(1, LANES), par(1, D_A),
                  *state_specs[4:6],
                  par(1, ZB_W), par(SUBLANES, D_B), par(LANES, D_B), par(LANES, D_B), par(R_G, D_B),
                  *state_specs[6:], par(1, gp), par(1, gp), par(1, gp),
                  par(D_C, gp), par(D_C, gp), par(gp, D_C), par(gp, D_C), par(1, D_C), par(D_C, D_C)],
        out_specs=(pl.BlockSpec((bb, ls, D_MIX), lambda i, j: (i, j, 0)),) + state_specs,
        scratch_shapes=[pltpu.VMEM((2, bb, ls, Z_N), F32), pltpu.VMEM((bb, ls + SUBLANES, 2 * D_A), F32),
                        pltpu.VMEM((D_C, gp), wdt), pltpu.VMEM((D_C, gp), wdt)],
        compiler_params=pltpu.CompilerParams(dimension_semantics=("arbitrary", "arbitrary"),
                                             vmem_limit_bytes=VMEM_LIMIT),
        name="mixers",
    )(x3, x3, p['w_in'], c0, n0p, m0p, cv0p, cwp, bi, bf, p['mlstm_norm_g'].reshape(1, D_A),
      s0, sh0p, mup, vecp, wupp, aupp, p['rwkv_g_up'].astype(wdt),
      hr0.reshape(b, 1, gp), hi0.reshape(b, 1, gp), flat(p['s5_A_re']), flat(p['s5_A_im']), ldt,
      bre, bim, cre.astype(wdt), cim.astype(wdt), p['s5_D'].reshape(1, D_C), p['s5_w_glu'].astype(wdt))
    states = (c1, n1[:, :H_A], m1[:, 0, :H_A], cv1[:, SUBLANES - (CONV_W - 1):],
              s1, _unpad_b_cols(sh1[:, SUBLANES - 1:]), hr1.reshape(b, G_C, P_C), hi1.reshape(b, G_C, P_C))
    return y, states


def _layer_norm(x, g, b):
    mu = jnp.mean(x, axis=-1, keepdims=True)
    dev = x - mu
    var = jnp.mean(dev * dev, axis=-1, keepdims=True)
    return dev * lax.rsqrt(var + LN_EPS) * g + b


def _argmax_lane(x, lane):
    m = jnp.max(x, axis=-1, keepdims=True)
    idx = jnp.min(jnp.where(x == m, lane, LANES), axis=-1, keepdims=True)
    return m, idx


def _out_proj_kernel(x_ref, y_ref, w_ref, g_ref, b_ref, wr_hi_ref, wr_lo_ref, br_ref, x1_ref, rt_ref, cnt_ref):
    @pl.when(pl.program_id(0) == 0)
    def _():
        cnt_ref[...] = jnp.zeros_like(cnt_ref)

    mix = _mm(y_ref[...], w_ref[...])
    x1 = _layer_norm(DEEP_ALPHA * x_ref[...] + mix, g_ref[...], b_ref[...])
    x1_ref[...] = x1
    tm = x1.shape[0]
    lane = lax.broadcasted_iota(jnp.int32, (tm, LANES), 1)
    x_hi = _bf(x1)
    x_lo = _bf(x1 - x_hi.astype(F32))
    fine = _mm(x_hi, wr_hi_ref[...])
    coarse = fine + _mm(x_hi, wr_lo_ref[...]) + _mm(x_lo, wr_hi_ref[...])
    logit = jnp.where(lane < N_GROUPS, coarse, fine) + br_ref[...]
    lco = jnp.where(lane < N_GROUPS, logit, NEG)
    mco, grp = _argmax_lane(lco, lane)
    p_grp = 1.0 / jnp.sum(jnp.exp(lco - mco), axis=-1, keepdims=True)
    lo_lane = N_GROUPS + grp * EXP_PER_GROUP
    lfi = jnp.where((lane >= lo_lane) & (lane < lo_lane + EXP_PER_GROUP), logit, NEG)
    m0, l0 = _argmax_lane(lfi, lane)
    m1, l1 = _argmax_lane(jnp.where(lane == l0, NEG, lfi), lane)
    t1 = jnp.exp(m1 - m0)
    g0 = p_grp / (1.0 + t1)
    g1 = p_grp * t1 / (1.0 + t1)
    onehot = jnp.where((lane == l0) | (lane == l1), 1.0, 0.0)
    before = _mm(_tril(tm, strict=True).astype(BF16), _bf(onehot)) + cnt_ref[...]
    rank0 = jnp.sum(jnp.where(lane == l0, before, 0.0), axis=-1, keepdims=True)
    rank1 = jnp.sum(jnp.where(lane == l1, before, 0.0), axis=-1, keepdims=True)
    cnt_ref[...] += jnp.sum(onehot, axis=0, keepdims=True)
    vals = (l0 - N_GROUPS).astype(F32), (l1 - N_GROUPS).astype(F32), g0, g1, rank0, rank1
    out = jnp.zeros((tm, LANES), F32)
    for j, val in enumerate(vals):
        out = jnp.where(lane == j, val, out)
    rt_ref[...] = out


def out_proj_router(x2d, ymix, w_out_bf16, ln_g, ln_b, w_coarse, b_coarse, w_fine, b_fine):
    t = x2d.shape[0]
    tm = min(TM, t)
    wr = jnp.concatenate([w_coarse, w_fine, jnp.zeros((D_MODEL, LANES - N_GROUPS - N_EXP), F32)], axis=1)
    wr_hi = wr.astype(BF16)
    wr_lo = (wr - wr_hi.astype(F32)).astype(BF16)
    br = jnp.concatenate([b_coarse, b_fine, jnp.zeros((LANES - N_GROUPS - N_EXP,), F32)]).reshape(1, LANES)
    tok = lambda w: pl.BlockSpec((tm, w), lambda i: (i, 0))
    par = lambda *shape: pl.BlockSpec(shape, lambda i: (0,) * len(shape))
    x1, route, cnt = pl.pallas_call(
        _out_proj_kernel,
        out_shape=(jax.ShapeDtypeStruct((t, D_MODEL), F32), jax.ShapeDtypeStruct((t, LANES), F32),
                   jax.ShapeDtypeStruct((1, LANES), F32)),
        grid=(t // tm,),
        in_specs=[tok(D_MODEL), tok(D_MIX), par(D_MIX, D_MODEL), par(1, D_MODEL), par(1, D_MODEL),
                  par(D_MODEL, LANES), par(D_MODEL, LANES), par(1, LANES)],
        out_specs=(tok(D_MODEL), tok(LANES), par(1, LANES)),
        compiler_params=pltpu.CompilerParams(dimension_semantics=("arbitrary",), vmem_limit_bytes=VMEM_LIMIT),
        name="out_proj_router",
    )(x2d, ymix, w_out_bf16, ln_g.reshape(1, D_MODEL), ln_b.reshape(1, D_MODEL), wr_hi, wr_lo, br)
    return x1, route, cnt[0, N_GROUPS:N_GROUPS + N_EXP]


def _row_copy(src_hbm, dst, sem, tok, r):
    return pltpu.make_async_copy(src_hbm.at[pl.ds(tok, 1)], dst.at[pl.ds(r, 1)], sem)


def _dispatch_kernel(d_ref, zb_ref, x1_ref, xs_out, zbuf, sem, zsem, *, tm):
    @pl.when(pl.program_id(0) == 0)
    def _():
        zbuf[...] = jnp.zeros_like(zbuf)
        fill = lambda k: pltpu.make_async_copy(zbuf, xs_out.at[pl.ds(zb_ref[k] * MOE_BLOCK, MOE_BLOCK)], zsem)
        for k in range(zb_ref.shape[0]):
            pl.when(zb_ref[k] >= 0)(lambda k=k: fill(k).start())
        for k in range(zb_ref.shape[0]):
            pl.when(zb_ref[k] >= 0)(lambda k=k: fill(k).wait())

    base = pl.program_id(0) * (tm * TOP_K)
    for r in range(tm):
        for j in range(TOP_K):
            pltpu.make_async_copy(x1_ref.at[pl.ds(r, 1)], xs_out.at[pl.ds(d_ref[base + r * TOP_K + j], 1)],
                                  sem).start(priority=j % 2)
    for j in range(TOP_K):
        pltpu.make_async_copy(x1_ref, xs_out.at[pl.ds(0, tm)], sem).wait()


def dispatch_rows(x1, dest, zero_blocks, n_rows):
    t = x1.shape[0]
    tm = min(TM, t)
    grid_spec = pltpu.PrefetchScalarGridSpec(
        num_scalar_prefetch=2,
        grid=(t // tm,),
        in_specs=[pl.BlockSpec((tm, D_MODEL), lambda i, d, zb: (i, 0))],
        out_specs=pl.BlockSpec(memory_space=pl.ANY),
        scratch_shapes=[pltpu.VMEM((MOE_BLOCK, D_MODEL), F32), pltpu.SemaphoreType.DMA, pltpu.SemaphoreType.DMA],
    )
    return pl.pallas_call(
        functools.partial(_dispatch_kernel, tm=tm),
        out_shape=jax.ShapeDtypeStruct((n_rows, D_MODEL), F32),
        grid_spec=grid_spec,
        compiler_params=pltpu.CompilerParams(dimension_semantics=("arbitrary",), vmem_limit_bytes=VMEM_LIMIT,
                                             has_side_effects=True),
        name="dispatch_rows",
    )(dest, zero_blocks, x1)


def _ffn_kernel(be_ref, xs_ref, wg_ref, wu_ref, wd_ref, ys_ref, wg_b, wu_b, wd_b):
    i = pl.program_id(0)

    @pl.when((i == 0) | (be_ref[i] != be_ref[jnp.maximum(i - 1, 0)]))
    def _():
        wg_b[...] = _bf(wg_ref[...])
        wu_b[...] = _bf(wu_ref[...])
        wd_b[...] = _bf(wd_ref[...])

    xb = xs_ref[...].astype(BF16)
    hg = _mm(xb, wg_b[...])
    hu = _mm(xb, wu_b[...])
    h = (hg * _sigmoid(hg) * hu).astype(BF16)
    ys_ref[...] = _mm(h, wd_b[...])


def expert_ffn(xs, block_e, wg, wu, wd, layer):
    n_rows = xs.shape[0]
    grid_spec = pltpu.PrefetchScalarGridSpec(
        num_scalar_prefetch=1,
        grid=(n_rows // MOE_BLOCK,),
        in_specs=[pl.BlockSpec((MOE_BLOCK, D_MODEL), lambda i, be: (i, 0)),
                  pl.BlockSpec((None, None, D_MODEL, D_EXPERT), lambda i, be: (layer, be[i], 0, 0)),
                  pl.BlockSpec((None, None, D_MODEL, D_EXPERT), lambda i, be: (layer, be[i], 0, 0)),
                  pl.BlockSpec((None, None, D_EXPERT, D_MODEL), lambda i, be: (layer, be[i], 0, 0))],
        out_specs=pl.BlockSpec((MOE_BLOCK, D_MODEL), lambda i, be: (i, 0)),
        scratch_shapes=[pltpu.VMEM((D_MODEL, D_EXPERT), BF16), pltpu.VMEM((D_MODEL, D_EXPERT), BF16),
                        pltpu.VMEM((D_EXPERT, D_MODEL), BF16)],
    )
    return pl.pallas_call(
        _ffn_kernel,
        out_shape=jax.ShapeDtypeStruct((n_rows, D_MODEL), F32),
        grid_spec=grid_spec,
        compiler_params=pltpu.CompilerParams(dimension_semantics=("arbitrary",), vmem_limit_bytes=VMEM_LIMIT),
        name="expert_ffn",
    )(block_e, xs, wg, wu, wd)


def _combine_kernel(d_ref, x1_ref, rt_ref, g_ref, b_ref, ys_hbm, o_ref, buf, sem, *, tm):
    i = pl.program_id(0)
    n = pl.num_programs(0)
    slot = i % GATHER_SLOTS

    def gather(tile, slot_):
        base = tile * (tm * TOP_K)
        for r in range(tm):
            for j in range(TOP_K):
                _row_copy(ys_hbm, buf.at[slot_, j], sem.at[slot_], d_ref[base + r * TOP_K + j], r).start(
                    priority=j % 2)

    def wait_tile(slot_):
        for j in range(TOP_K):
            pltpu.make_async_copy(ys_hbm.at[pl.ds(0, tm)], buf.at[slot_, j], sem.at[slot_]).wait()

    @pl.when(i == 0)
    def _():
        for j in range(GATHER_SLOTS - 1):
            gather(jnp.minimum(j, n - 1), j)

    ahead = i + (GATHER_SLOTS - 1)
    gather(jnp.minimum(ahead, n - 1), ahead % GATHER_SLOTS)
    wait_tile(slot)

    rt = rt_ref[...]
    ffn = rt[:, 2:3] * buf[slot, 0] + rt[:, 3:4] * buf[slot, 1]
    o_ref[...] = _layer_norm(DEEP_ALPHA * x1_ref[...] + ffn, g_ref[...], b_ref[...])

    @pl.when(i == n - 1)
    def _():
        for j in range(1, GATHER_SLOTS):
            wait_tile((i + j) % GATHER_SLOTS)


def combine_ln(x1, route, dest, ys, ln_g, ln_b):
    t = x1.shape[0]
    tm = min(TM, t)
    grid_spec = pltpu.PrefetchScalarGridSpec(
        num_scalar_prefetch=1,
        grid=(t // tm,),
        in_specs=[pl.BlockSpec((tm, D_MODEL), lambda i, d: (i, 0)),
                  pl.BlockSpec((tm, LANES), lambda i, d: (i, 0)),
                  pl.BlockSpec((1, D_MODEL), lambda i, d: (0, 0)),
                  pl.BlockSpec((1, D_MODEL), lambda i, d: (0, 0)),
                  pl.BlockSpec(memory_space=pl.ANY)],
        out_specs=pl.BlockSpec((tm, D_MODEL), lambda i, d: (i, 0)),
        scratch_shapes=[pltpu.VMEM((GATHER_SLOTS, TOP_K, tm, D_MODEL), F32),
                        pltpu.SemaphoreType.DMA((GATHER_SLOTS,))],
    )
    return pl.pallas_call(
        functools.partial(_combine_kernel, tm=tm),
        out_shape=jax.ShapeDtypeStruct((t, D_MODEL), F32),
        grid_spec=grid_spec,
        compiler_params=pltpu.CompilerParams(dimension_semantics=("arbitrary",), vmem_limit_bytes=VMEM_LIMIT),
        name="combine_ln",
    )(dest, x1, route, ln_g.reshape(1, D_MODEL), ln_b.reshape(1, D_MODEL), ys)


def route_tables(route, counts):
    t = route.shape[0]
    n_blocks = -(-(t * TOP_K) // MOE_BLOCK) + N_EXP
    counts = counts.astype(jnp.int32)
    padded = (counts + MOE_BLOCK - 1) // MOE_BLOCK * MOE_BLOCK
    pad_end = jnp.cumsum(padded)
    pad_start = pad_end - padded
    flat_e = route[:, :TOP_K].astype(jnp.int32).reshape(-1)
    dest = pad_start[flat_e] + route[:, 4:4 + TOP_K].astype(jnp.int32).reshape(-1)
    first_row = jnp.arange(n_blocks, dtype=jnp.int32) * MOE_BLOCK
    block_e = jnp.minimum(jnp.sum((pad_end[None, :] <= first_row[:, None]).astype(jnp.int32), axis=1), N_EXP - 1)
    last_blk = jnp.where(padded > counts, pad_end // MOE_BLOCK - 1, -1)
    tail_blk = pad_end[-1] // MOE_BLOCK + jnp.arange(N_EXP, dtype=jnp.int32)
    zero_blocks = jnp.concatenate([last_blk, jnp.where(tail_blk < n_blocks, tail_blk, -1)]).astype(jnp.int32)
    return block_e, dest, zero_blocks, n_blocks * MOE_BLOCK


def _prep_w_in(w_in):
    wa = w_in[:, :ZA_W]
    gi = w_in[:, ZA_W:ZA_W + H_A]
    gf = w_in[:, ZA_W + H_A:N_A_COLS]
    wb = _pad_b_cols(w_in[:, N_A_COLS:N_A_COLS + N_B_COLS])
    wc = w_in[:, N_A_COLS + N_B_COLS:]
    gpad = jnp.zeros((D_MODEL, LANES - H_A), w_in.dtype)
    return jnp.concatenate([wa, wb, gi, gpad, gf, gpad, wc], axis=1).astype(BF16)


def _trunk_layer(x3, st, npad, p, experts, layer, exact):
    b, lp, _ = x3.shape
    x2d = x3.reshape(b * lp, D_MODEL)
    ymix, st_new = mixers(x3, st, p, npad, exact)
    x1, route, counts = out_proj_router(x2d, ymix.reshape(b * lp, D_MIX), p['w_out'],
                                        p['ln_g'][0], p['ln_b'][0], p['moe_w_coarse'], p['moe_b_coarse'],
                                        p['moe_w_fine'], p['moe_b_fine'])
    block_e, dest, zero_blocks, n_rows = route_tables(route, counts)
    xs = dispatch_rows(x1, dest, zero_blocks, n_rows)
    ys = expert_ffn(xs, block_e, *experts, layer)
    x2 = combine_ln(x1, route, dest, ys, p['ln_g'][1], p['ln_b'][1])
    return x2.reshape(b, lp, D_MODEL), st_new


def _run_trunk(x3, states, npad, params, experts, exact):
    new = [[] for _ in states]
    for l in range(DEPTH):
        p = {name: arr[l] for name, arr in params.items()}
        x3, st_new = _trunk_layer(x3, [s[l].astype(F32) for s in states], npad, p, experts, l, exact)
        for lst, s in zip(new, st_new):
            lst.append(s)
    return x3, [jnp.stack(lst) for lst in new]


def kernel(x_prompt, x_sample, state_mlstm_C, state_mlstm_n, state_mlstm_m, state_mlstm_conv, state_rwkv_S, state_rwkv_shift, state_s5_re, state_s5_im, meta_tokens, w_in, w_out, mlstm_conv_w, mlstm_gate_b, mlstm_norm_g, rwkv_mu, rwkv_vecs, rwkv_w_up, rwkv_a_up, rwkv_g_up, s5_A_re, s5_A_im, s5_log_dt, s5_B_re, s5_B_im, s5_C_re, s5_C_im, s5_D, s5_w_glu, ln_g, ln_b, moe_w_coarse, moe_b_coarse, moe_w_fine, moe_b_fine, moe_w_gate, moe_w_up, moe_w_down):
    params = dict(w_in=jax.vmap(_prep_w_in)(w_in), w_out=w_out.astype(BF16), mlstm_conv_w=mlstm_conv_w,
                  mlstm_gate_b=mlstm_gate_b, mlstm_norm_g=mlstm_norm_g, rwkv_mu=rwkv_mu, rwkv_vecs=rwkv_vecs,
                  rwkv_w_up=rwkv_w_up, rwkv_a_up=rwkv_a_up, rwkv_g_up=rwkv_g_up, s5_A_re=s5_A_re, s5_A_im=s5_A_im,
                  s5_log_dt=s5_log_dt, s5_B_re=s5_B_re, s5_B_im=s5_B_im, s5_C_re=s5_C_re, s5_C_im=s5_C_im,
                  s5_D=s5_D, s5_w_glu=s5_w_glu, ln_g=ln_g, ln_b=ln_b, moe_w_coarse=moe_w_coarse,
                  moe_b_coarse=moe_b_coarse, moe_w_fine=moe_w_fine, moe_b_fine=moe_b_fine,
                  )
    experts = (moe_w_gate, moe_w_up, moe_w_down)
    state_in = [state_mlstm_C, state_mlstm_n, state_mlstm_m, state_mlstm_conv,
                state_rwkv_S, state_rwkv_shift, state_s5_re, state_s5_im]

    b, seq, _ = x_prompt.shape
    real = N_META + seq
    npad = (-real) % SEQ_PAD_MULT
    meta = jnp.broadcast_to(meta_tokens.astype(x_prompt.dtype)[None], (b, N_META, D_MODEL))
    xp = jnp.concatenate([jnp.zeros((b, npad, D_MODEL), x_prompt.dtype), meta, x_prompt], axis=1)
    fresh = [jnp.zeros((DEPTH, b) + s.shape[2:], F32) for s in state_in]
    yp, pst = _run_trunk(xp, fresh, npad, params, experts, exact=False)
    y_prompt = yp[:, npad + N_META:]
    pst = [s.astype(r.dtype) for s, r in zip(pst, state_in)]

    y_sample, sst = _run_trunk(x_sample, state_in, 0, params, experts, exact=True)
    sst = [s.astype(r.dtype) for s, r in zip(sst, state_in)]
    return (y_prompt, y_sample, *pst, *sst)
```
